```python
import jax, jax.numpy as jnp
from jax import lax
import numpy as np

D_MODEL = 2048
BATCH = 8
SEQ = 4096
DEPTH = 4

CTX_LEN = 256
GRID_W = 64
HEAD_DIM = 128
N_HEADS_A = 8
N_KV_A = 2
N_HEADS_B = 8
Q_BLOCK = 128
NA_KH = 8
NA_KW = 16
D_FF = 5632
N_EXPERTS = 8
TOP_K = 2
ROPE_THETA = 10000.0
EPS = 1e-6
N_MOD = 6
WA_Q = N_HEADS_A * HEAD_DIM
WA_KV = N_KV_A * HEAD_DIM
WB = N_HEADS_B * HEAD_DIM
MIX_W = WA_Q + WB
Q_COLS = WA_Q + WB
IN_COLS = Q_COLS + 2 * WA_KV + 2 * WB
N_DENSE = (DEPTH + 1) // 2
N_MOE = DEPTH // 2

kernel_name = 'hybrid_dit_gqa_natten_moe'


def rms_norm(x, g):
    xf = x.astype(jnp.float32)
    y = xf * lax.rsqrt(jnp.mean(xf * xf, axis=-1, keepdims=True) + EPS)
    return (y * g.astype(jnp.float32)).astype(x.dtype)


def modulate(h, shift, scale):
    return h * (1 + scale) + shift


def rope_1d(x, pos):
    half = x.shape[-1] // 2
    freqs = ROPE_THETA ** (-jnp.arange(half, dtype=jnp.float32) / half)
    ang = pos.astype(jnp.float32)[:, None] * freqs[None, :]
    cos = jnp.cos(ang)[:, None, :].astype(x.dtype)
    sin = jnp.sin(ang)[:, None, :].astype(x.dtype)
    x1, x2 = x[..., :half], x[..., half:]
    return jnp.concatenate([x1 * cos - x2 * sin, x2 * cos + x1 * sin], axis=-1)


def rope_2d(x, pos_r, pos_c):
    half = x.shape[-1] // 2
    return jnp.concatenate([rope_1d(x[..., :half], pos_r), rope_1d(x[..., half:], pos_c)], axis=-1)


def split_heads(p, n_heads):
    return p.reshape(*p.shape[:-1], n_heads, HEAD_DIM)


def split_q(p):
    return split_heads(p[..., :WA_Q], N_HEADS_A), split_heads(p[..., WA_Q:], N_HEADS_B)


def split_kv(p):
    o1, o2, o3 = WA_KV, 2 * WA_KV, 2 * WA_KV + WB
    return (split_heads(p[..., :o1], N_KV_A), split_heads(p[..., o1:o2], N_KV_A),
            split_heads(p[..., o2:o3], N_HEADS_B), split_heads(p[..., o3:], N_HEADS_B))


def gqa_attend(q, k, v):
    b, sq, h, d = q.shape
    hk = k.shape[2]
    qg = q.reshape(b, sq, hk, h // hk, d)
    s = jnp.einsum('bqkgd,bskd->bkgqs', qg, k).astype(jnp.float32) * (d ** -0.5)
    p = jax.nn.softmax(s, axis=-1).astype(v.dtype)
    o = jnp.einsum('bkgqs,bskd->bqkgd', p, v)
    return o.reshape(b, sq, h * d)


def global_gqa_blocks(q, k_all, v_all):
    b, s, h, d = q.shape
    nb = s // Q_BLOCK
    q_blocks = q.reshape(b, nb, Q_BLOCK, h, d).transpose(1, 0, 2, 3, 4)
    o = lax.map(lambda qb: gqa_attend(qb, k_all, v_all), q_blocks)
    return o.transpose(1, 0, 2, 3).reshape(b, s, h * d)


def neighbourhood_attend(q, k, v, k_ctx, v_ctx, rpb, rows):
    b, s, h, d = q.shape
    kh = min(NA_KH, rows)
    kk = kh * NA_KW
    r = jnp.arange(rows)
    col = jnp.arange(GRID_W)
    r0 = jnp.clip(r - kh // 2, 0, rows - kh)
    c0 = jnp.clip(col - NA_KW // 2, 0, GRID_W - NA_KW)
    key_r = r0[:, None] + jnp.arange(kh)[None, :]
    key_c = c0[:, None] + jnp.arange(NA_KW)[None, :]
    key_idx = (key_r[:, None, :, None] * GRID_W + key_c[None, :, None, :]).reshape(rows, GRID_W, kk)
    dr = key_r - r[:, None] + (NA_KH - 1)
    dc = key_c - col[:, None] + (NA_KW - 1)
    bias = rpb[:, dr[:, None, :, None], dc[None, :, None, :]]
    bias = bias.reshape(h, rows, GRID_W, kk).transpose(1, 0, 2, 3).astype(jnp.float32)
    q_rows = q.reshape(b, rows, GRID_W, h, d).transpose(1, 0, 2, 3, 4)
    scale = d ** -0.5

    def row_block(args):
        qr, idx, br = args
        kg = jnp.take(k, idx, axis=1)
        vg = jnp.take(v, idx, axis=1)
        s_loc = jnp.einsum('bwhd,bwkhd->bhwk', qr, kg).astype(jnp.float32) * scale + br
        s_ctx = jnp.einsum('bwhd,bchd->bhwc', qr, k_ctx).astype(jnp.float32) * scale
        p = jax.nn.softmax(jnp.concatenate([s_loc, s_ctx], axis=-1), axis=-1).astype(v.dtype)
        o = (jnp.einsum('bhwk,bwkhd->bwhd', p[..., :kk], vg)
             + jnp.einsum('bhwc,bchd->bwhd', p[..., kk:], v_ctx))
        return o.reshape(b, GRID_W, h * d)

    o = lax.map(row_block, (q_rows, key_idx, bias))
    return o.transpose(1, 0, 2, 3).reshape(b, s, h * d)


def swiglu(h, w1, w3, w2):
    return (jax.nn.silu(h @ w1) * (h @ w3)) @ w2


def moe_swiglu(h, w_router, w1, w3, w2):
    logits = (h @ w_router).astype(jnp.float32)
    top_v, top_i = lax.top_k(logits, TOP_K)
    top_w = jax.nn.softmax(top_v, axis=-1)
    gates = jnp.sum(jax.nn.one_hot(top_i, N_EXPERTS, dtype=jnp.float32) * top_w[..., None], axis=-2).astype(h.dtype)
    y = jnp.zeros_like(h)
    for e in range(N_EXPERTS):
        y = y + gates[..., e:e + 1] * swiglu(h, w1[e], w3[e], w2[e])
    return y


def channel_mixer(h, i, w1_dense, w3_dense, w2_dense, w_router, w1_moe, w3_moe, w2_moe):
    j = i // 2
    if i % 2 == 0:
        return swiglu(h, w1_dense[j], w3_dense[j], w2_dense[j])
    return moe_swiglu(h, w_router[j], w1_moe[j], w3_moe[j], w2_moe[j])


def setup_inputs(seed: int = 0) -> dict:
    key = jax.random.key(seed)
    ks = jax.random.split(key, 24)
    f32 = jnp.float32
    D = D_MODEL

    def nrm(k, shape, scale):
        return jax.random.normal(k, shape, f32) * scale

    return {
        'x': nrm(ks[0], (BATCH, SEQ, D), 1.0),
        'c': nrm(ks[1], (BATCH, D), 1.0),
        'ctx': nrm(ks[2], (BATCH, CTX_LEN, D), 1.0),
        'c_ctx': nrm(ks[3], (D,), 1.0),
        'ada_w': nrm(ks[4], (DEPTH, D, N_MOD * D), 0.5 * D ** -0.5),
        'ada_b': nrm(ks[5], (DEPTH, N_MOD * D), 0.01),
        'norm_attn': 1.0 + nrm(ks[6], (DEPTH, D), 0.01),
        'norm_ffn': 1.0 + nrm(ks[7], (DEPTH, D), 0.01),
        'w_in': nrm(ks[8], (DEPTH, D, IN_COLS), D ** -0.5),
        'qn_a': 1.0 + nrm(ks[9], (DEPTH, HEAD_DIM), 0.01),
        'kn_a': 1.0 + nrm(ks[10], (DEPTH, HEAD_DIM), 0.01),
        'qn_b': 1.0 + nrm(ks[11], (DEPTH, HEAD_DIM), 0.01),
        'kn_b': 1.0 + nrm(ks[12], (DEPTH, HEAD_DIM), 0.01),
        'rpb': nrm(ks[13], (DEPTH, N_HEADS_B, 2 * NA_KH - 1, 2 * NA_KW - 1), 0.1),
        'w_out': nrm(ks[14], (DEPTH, MIX_W, D), MIX_W ** -0.5),
        'w1_dense': nrm(ks[15], (N_DENSE, D, D_FF), D ** -0.5),
        'w3_dense': nrm(ks[16], (N_DENSE, D, D_FF), D ** -0.5),
        'w2_dense': nrm(ks[17], (N_DENSE, D_FF, D), D_FF ** -0.5),
        'w_router': nrm(ks[18], (N_MOE, D, N_EXPERTS), D ** -0.5),
        'w1_moe': nrm(ks[19], (N_MOE, N_EXPERTS, D, D_FF), D ** -0.5),
        'w3_moe': nrm(ks[20], (N_MOE, N_EXPERTS, D, D_FF), D ** -0.5),
        'w2_moe': nrm(ks[21], (N_MOE, N_EXPERTS, D_FF, D), D_FF ** -0.5),
    }


def reference(x, c, ctx, c_ctx, ada_w, ada_b, norm_attn, norm_ffn, w_in, qn_a, kn_a, qn_b, kn_b,
              rpb, w_out, w1_dense, w3_dense, w2_dense, w_router, w1_moe, w3_moe, w2_moe):
    b, s, _ = x.shape
    rows = s // GRID_W
    t = jnp.arange(s)
    pos_r, pos_c = t // GRID_W, t % GRID_W
    cond = jax.nn.silu(jnp.concatenate([c, c_ctx[None, :]], axis=0))
    for i in range(DEPTH):
        last = i == DEPTH - 1
        mod = (cond @ ada_w[i] + ada_b[i])[:, None, :]
        ml = jnp.split(mod[:b], N_MOD, axis=-1)
        mc = jnp.split(mod[b:], N_MOD, axis=-1)
        w_q, w_kv = w_in[i][:, :Q_COLS], w_in[i][:, Q_COLS:]

        h = modulate(rms_norm(x, norm_attn[i]), ml[0], ml[1])
        hc = modulate(rms_norm(ctx, norm_attn[i]), mc[0], mc[1])
        qa, qb = split_q(h @ w_q)
        ka, va, kb, vb = split_kv(h @ w_kv)
        ka_c, va_c, kb_c, vb_c = split_kv(hc @ w_kv)
        qa = rope_2d(rms_norm(qa, qn_a[i]), pos_r, pos_c)
        ka = rope_2d(rms_norm(ka, kn_a[i]), pos_r, pos_c)
        qb = rms_norm(qb, qn_b[i])
        kb = rms_norm(kb, kn_b[i])
        ka_c = rms_norm(ka_c, kn_a[i])
        kb_c = rms_norm(kb_c, kn_b[i])
        k_all = jnp.concatenate([ka_c, ka], axis=1)
        v_all = jnp.concatenate([va_c, va], axis=1)
        o_a = global_gqa_blocks(qa, k_all, v_all)
        o_b = neighbourhood_attend(qb, kb, vb, kb_c, vb_c, rpb[i], rows)
        x = x + ml[2] * (jnp.concatenate([o_a, o_b], axis=-1) @ w_out[i])
        if not last:
            qa_c, qb_c = split_q(hc @ w_q)
            qa_c = rms_norm(qa_c, qn_a[i])
            qb_c = rms_norm(qb_c, qn_b[i])
            o_c = jnp.concatenate([gqa_attend(qa_c, ka_c, va_c), gqa_attend(qb_c, kb_c, vb_c)], axis=-1)
            ctx = ctx + mc[2] * (o_c @ w_out[i])

        h = modulate(rms_norm(x, norm_ffn[i]), ml[3], ml[4])
        x = x + ml[5] * channel_mixer(h, i, w1_dense, w3_dense, w2_dense, w_router, w1_moe, w3_moe, w2_moe)
        if not last:
            hc = modulate(rms_norm(ctx, norm_ffn[i]), mc[3], mc[4])
            ctx = ctx + mc[5] * channel_mixer(hc, i, w1_dense, w3_dense, w2_dense, w_router, w1_moe, w3_moe, w2_moe)
    return x
```

```python
import functools
from typing import NamedTuple

import numpy as np
import jax
import jax.numpy as jnp
from jax import lax
from jax.experimental import pallas as pl
from jax.experimental.pallas import tpu as pltpu

F32 = jnp.float32
BF16 = jnp.bfloat16

EPS = 1e-6
ROPE_THETA = 10000.0
N_MOD = 6
LANE = 128
MASK_VALUE = -1e30
V7X_VMEM_LIMIT = 56 * 1024 * 1024


class Cfg(NamedTuple):
    d_model: int
    batch: int
    seq: int
    ctx_len: int
    grid_w: int
    head_dim: int
    heads_a: int
    kv_a: int
    heads_b: int
    na_kh: int
    na_kw: int
    d_ff: int
    n_experts: int
    depth: int

    @property
    def wa_q(self):
        return self.heads_a * self.head_dim

    @property
    def wa_kv(self):
        return self.kv_a * self.head_dim

    @property
    def wb(self):
        return self.heads_b * self.head_dim

    @property
    def mix_w(self):
        return self.wa_q + self.wb

    @property
    def in_cols(self):
        return self.mix_w + 2 * self.wa_kv + 2 * self.wb

    @property
    def rows(self):
        return self.seq // self.grid_w

    @property
    def n_lat(self):
        return self.batch * self.seq

    @property
    def n_tok(self):
        return self.batch * (self.seq + self.ctx_len)

    @property
    def off_ka(self):
        return self.mix_w

    @property
    def off_va(self):
        return self.mix_w + self.wa_kv

    @property
    def off_kb(self):
        return self.mix_w + 2 * self.wa_kv

    @property
    def off_vb(self):
        return self.mix_w + 2 * self.wa_kv + self.wb


class Tiles(NamedTuple):
    tm: int
    tn_proj: int
    tn_out: int
    tf: int
    tq: int
    rb: int
    tn_ada: int


def _pick_tiles(cfg):
    n_ctx = cfg.batch * cfg.ctx_len
    tm = min(512, cfg.seq, n_ctx)
    assert cfg.seq % tm == 0 and n_ctx % tm == 0
    tn_proj = 2 * cfg.head_dim
    for w in (cfg.wa_q, cfg.wb, cfg.wa_kv):
        assert w % tn_proj == 0
    tn_out = min(512, cfg.d_model)
    tf = min(512, cfg.d_ff)
    assert cfg.d_model % tn_out == 0 and cfg.d_ff % tf == 0
    tq = min(256, cfg.seq)
    assert cfg.seq % tq == 0
    rb = 4
    assert cfg.rows % rb == 0 and cfg.rows >= rb + cfg.na_kh
    tn_ada = next(t for t in (1024, 512, 256, LANE) if (N_MOD * cfg.d_model) % t == 0)
    return Tiles(tm, tn_proj, tn_out, tf, tq, rb, tn_ada)


def _params(*sem):
    return pltpu.CompilerParams(dimension_semantics=sem, vmem_limit_bytes=V7X_VMEM_LIMIT)


def _dot(a, b):
    return jnp.dot(a, b, preferred_element_type=F32)


def _dot_nt(a, b):
    return lax.dot_general(a, b, (((1,), (1,)), ((), ())), preferred_element_type=F32)


def _rms(x):
    return x * lax.rsqrt(jnp.mean(x * x, axis=-1, keepdims=True) + EPS)


def _ada_kernel(cond_ref, w_ref, b_ref, o_ref):
    c = cond_ref[...]
    c = (c * jax.nn.sigmoid(c)).astype(BF16)
    o_ref[0] = _dot(c, w_ref[0].astype(BF16)) + b_ref[0]


def _ada_mod(cfg, tl, cond_pad, ada_w, ada_b):
    depth, d, n6 = ada_w.shape
    mp = cond_pad.shape[0]
    return pl.pallas_call(
        _ada_kernel,
        grid=(depth, n6 // tl.tn_ada),
        in_specs=[
            pl.BlockSpec((mp, d), lambda l, j: (0, 0)),
            pl.BlockSpec((1, d, tl.tn_ada), lambda l, j: (l, 0, j)),
            pl.BlockSpec((1, 1, tl.tn_ada), lambda l, j: (l, 0, j)),
        ],
        out_specs=pl.BlockSpec((1, mp, tl.tn_ada), lambda l, j: (l, 0, j)),
        out_shape=jax.ShapeDtypeStruct((depth, mp, n6), F32),
        compiler_params=_params("parallel", "parallel"),
        name="ada_mod",
    )(cond_pad, ada_w, ada_b.reshape(depth, 1, n6))


def _mod_index(cfg, tm):
    n_lat_tiles = cfg.n_lat // tm
    per_sample = cfg.seq // tm

    def f(i):
        return jnp.where(i < n_lat_tiles, i // per_sample, cfg.batch)
    return f


def _mod_spec(cfg, tm, k, ncols, grid_rank, col_axis=None):
    mi = _mod_index(cfg, tm)
    if grid_rank == 2:
        if col_axis is None:
            return pl.BlockSpec((1, 1, 1, ncols), lambda i, j: (mi(i), k, 0, 0))
        return pl.BlockSpec((1, 1, 1, ncols), lambda i, j: (mi(i), k, 0, j))
    assert grid_rank == 3 and col_axis is None
    return pl.BlockSpec((1, 1, 1, ncols), lambda i, e, f: (mi(i), k, 0, 0))


def _modulated(x, g_ref, shift_ref, scale_ref):
    y = _rms(x) * g_ref[...]
    return y * (1.0 + scale_ref[0, 0]) + shift_ref[0, 0]


def _swap_halves(y):
    lane = lax.broadcasted_iota(jnp.int32, y.shape, 1)
    quarter = LANE // 4
    fwd = pltpu.roll(y, LANE - quarter, 1)
    bwd = pltpu.roll(y, quarter, 1)
    return jnp.where((lane % (2 * quarter)) < quarter, fwd, bwd)


def _proj_kernel(x_ref, g_ref, shift_ref, scale_ref, w_ref, gain_ref, cos_ref, sin_ref, o_ref, h_ref,
                 *, rope_tiles, norm_tiles, heads_per_tile):
    j = pl.program_id(1)

    @pl.when(j == 0)
    def _():
        h_ref[...] = _modulated(x_ref[...], g_ref, shift_ref, scale_ref).astype(BF16)

    acc = _dot(h_ref[...], w_ref[...])

    def in_ranges(ranges):
        cond = None
        for lo, hi in ranges:
            c = jnp.logical_and(j >= lo, j < hi)
            cond = c if cond is None else jnp.logical_or(cond, c)
        return cond

    is_rope = in_ranges(rope_tiles)
    is_norm = in_ranges(norm_tiles)

    def head(c):
        return acc[:, c * LANE:(c + 1) * LANE]

    @pl.when(is_rope)
    def _():
        for c in range(heads_per_tile):
            y = _rms(head(c)) * gain_ref[:, c * LANE:(c + 1) * LANE]
            y = y * cos_ref[...] + _swap_halves(y) * sin_ref[...]
            o_ref[:, c * LANE:(c + 1) * LANE] = y.astype(o_ref.dtype)

    @pl.when(is_norm)
    def _():
        for c in range(heads_per_tile):
            y = _rms(head(c)) * gain_ref[:, c * LANE:(c + 1) * LANE]
            o_ref[:, c * LANE:(c + 1) * LANE] = y.astype(o_ref.dtype)

    @pl.when(jnp.logical_not(jnp.logical_or(is_rope, is_norm)))
    def _():
        o_ref[...] = acc.astype(o_ref.dtype)


def _in_proj(cfg, tl, x_all, g, mod, w_in, gain, rope_cos, rope_sin):
    tm, tn = tl.tm, tl.tn_proj
    d = cfg.d_model
    n_tiles = cfg.n_tok // tm
    n_lat_tiles = cfg.n_lat // tm
    per_sample = cfg.seq // tm
    u = lambda cols: cols // tn
    qa = (0, u(cfg.wa_q))
    qb = (qa[1], qa[1] + u(cfg.wb))
    ka = (qb[1], qb[1] + u(cfg.wa_kv))
    va = (ka[1], ka[1] + u(cfg.wa_kv))
    kb = (va[1], va[1] + u(cfg.wb))
    kern = functools.partial(_proj_kernel, rope_tiles=(qa, ka), norm_tiles=(qb, kb),
                             heads_per_tile=tn // LANE)

    def rope_idx(i, j):
        return (jnp.where(i < n_lat_tiles, i % per_sample, per_sample), 0)

    return pl.pallas_call(
        kern,
        grid=(n_tiles, cfg.in_cols // tn),
        in_specs=[
            pl.BlockSpec((tm, d), lambda i, j: (i, 0)),
            pl.BlockSpec((1, d), lambda i, j: (0, 0)),
            _mod_spec(cfg, tm, 0, d, 2),
            _mod_spec(cfg, tm, 1, d, 2),
            pl.BlockSpec((d, tn), lambda i, j: (0, j)),
            pl.BlockSpec((1, tn), lambda i, j: (0, j)),
            pl.BlockSpec((tm, LANE), rope_idx),
            pl.BlockSpec((tm, LANE), rope_idx),
        ],
        out_specs=pl.BlockSpec((tm, tn), lambda i, j: (i, j)),
        out_shape=jax.ShapeDtypeStruct((cfg.n_tok, cfg.in_cols), BF16),
        scratch_shapes=[pltpu.VMEM((tm, d), BF16)],
        compiler_params=_params("parallel", "arbitrary"),
        name="in_proj",
    )(x_all, g, mod, mod, w_in, gain, rope_cos, rope_sin)


def _softmax_pv(s_parts, v_parts):
    m = None
    for s in s_parts:
        mx = jnp.max(s, axis=-1, keepdims=True)
        m = mx if m is None else jnp.maximum(m, mx)
    l = None
    o = None
    for s, v in zip(s_parts, v_parts):
        p = jnp.exp(s - m)
        ps = jnp.sum(p, axis=-1, keepdims=True)
        pv = _dot(p.astype(BF16), v)
        l = ps if l is None else l + ps
        o = pv if o is None else o + pv
    return o / l


def _attn_a_kernel(q_ref, kl_ref, vl_ref, kc_ref, vc_ref, o_ref, *, group):
    kl, vl, kc, vc = kl_ref[...], vl_ref[...], kc_ref[...], vc_ref[...]
    for h in range(group):
        q = q_ref[:, h * LANE:(h + 1) * LANE]
        o = _softmax_pv([_dot_nt(q, kl), _dot_nt(q, kc)], [vl, vc])
        o_ref[:, h * LANE:(h + 1) * LANE] = o.astype(o_ref.dtype)


def _attn_a(cfg, tl, p):
    assert cfg.head_dim == LANE
    group = cfg.heads_a // cfg.kv_a
    gw = group * LANE
    tq = tl.tq
    nq = cfg.seq // tq
    ctx_blk0 = cfg.n_lat // cfg.ctx_len
    ka0, va0 = cfg.off_ka // LANE, cfg.off_va // LANE
    return pl.pallas_call(
        functools.partial(_attn_a_kernel, group=group),
        grid=(cfg.batch, cfg.kv_a, nq),
        in_specs=[
            pl.BlockSpec((tq, gw), lambda b, g, i: (b * nq + i, g)),
            pl.BlockSpec((cfg.seq, LANE), lambda b, g, i: (b, ka0 + g)),
            pl.BlockSpec((cfg.seq, LANE), lambda b, g, i: (b, va0 + g)),
            pl.BlockSpec((cfg.ctx_len, LANE), lambda b, g, i: (ctx_blk0 + b, ka0 + g)),
            pl.BlockSpec((cfg.ctx_len, LANE), lambda b, g, i: (ctx_blk0 + b, va0 + g)),
        ],
        out_specs=pl.BlockSpec((tq, gw), lambda b, g, i: (b * nq + i, g)),
        out_shape=jax.ShapeDtypeStruct((cfg.n_tok, cfg.mix_w), BF16),
        compiler_params=_params("parallel", "parallel", "arbitrary"),
        name="attn_global",
    )(p, p, p, p, p)


def _na_geometry(cfg, rb):
    rows, w = cfg.rows, cfg.grid_w
    kh = min(cfg.na_kh, rows)
    kw = cfg.na_kw
    win = min(rows, -(-(rb - 1 + kh) // rb) * rb)
    nblk = rows // rb
    r0 = np.clip(np.arange(rows) - kh // 2, 0, rows - kh)
    c0 = np.clip(np.arange(w) - kw // 2, 0, w - kw)
    starts = np.clip(rb * np.arange(nblk) - kh // 2, 0, rows - win)

    def table(i):
        r = rb * i + np.arange(rb)
        key_r = starts[i] + np.arange(win)
        col = np.arange(w)
        vr = (key_r[None, :] >= r0[r][:, None]) & (key_r[None, :] < r0[r][:, None] + kh)
        vc = (col[None, :] >= c0[:, None]) & (col[None, :] < c0[:, None] + kw)
        dr = key_r[None, :] - r[:, None] + (cfg.na_kh - 1)
        dc = col[None, :] - col[:, None] + (cfg.na_kw - 1)
        valid = vr[:, None, :, None] & vc[None, :, None, :]
        drb = np.broadcast_to(dr[:, None, :, None], valid.shape)
        dcb = np.broadcast_to(dc[None, :, None, :], valid.shape)
        shp = (rb * w, win * w)
        return (valid.reshape(shp), np.where(valid, drb, 0).reshape(shp), np.where(valid, dcb, 0).reshape(shp))

    tables, classes = [], []
    for i in range(nblk):
        t = table(i)
        for c, u in enumerate(tables):
            if all(np.array_equal(a, b) for a, b in zip(t, u)):
                classes.append(c)
                break
        else:
            classes.append(len(tables))
            tables.append(t)
    return win, starts, np.asarray(classes), tables


def _na_bias(rpb_l, tables):
    out = []
    for valid, dr, dc in tables:
        b = rpb_l[:, dr, dc]
        out.append(jnp.where(valid[None], b, MASK_VALUE))
    return jnp.stack(out, axis=1).astype(F32)


def _attn_b_kernel(start_ref, cls_ref, o_in_ref, q_ref, kl_ref, vl_ref, kc_ref, vc_ref, bias_ref, o_ref,
                   *, nblk, qb, kwin, w):
    del o_in_ref
    kc, vc = kc_ref[...], vc_ref[...]

    def body(i, carry):
        k0 = pl.multiple_of(start_ref[i] * w, w)
        q = q_ref[pl.ds(pl.multiple_of(i * qb, qb), qb), :]
        kw_ = kl_ref[pl.ds(k0, kwin), :]
        vw = vl_ref[pl.ds(k0, kwin), :]
        s_loc = _dot_nt(q, kw_) + bias_ref[0, cls_ref[i]]
        o = _softmax_pv([s_loc, _dot_nt(q, kc)], [vw, vc])
        o_ref[pl.ds(pl.multiple_of(i * qb, qb), qb), :] = o.astype(o_ref.dtype)
        return carry

    lax.fori_loop(0, nblk, body, 0)


def _attn_b(cfg, tl, p, o, bias, starts, classes, win):
    w = cfg.grid_w
    qb, kwin = tl.rb * w, win * w
    nblk = cfg.rows // tl.rb
    ncls = bias.shape[1]
    ctx_blk0 = cfg.n_lat // cfg.ctx_len
    q0, k0, v0 = cfg.wa_q // LANE, cfg.off_kb // LANE, cfg.off_vb // LANE
    grid_spec = pltpu.PrefetchScalarGridSpec(
        num_scalar_prefetch=2,
        grid=(cfg.heads_b, cfg.batch),
        in_specs=[
            pl.BlockSpec(memory_space=pl.ANY),
            pl.BlockSpec((cfg.seq, LANE), lambda h, b, s, c: (b, q0 + h)),
            pl.BlockSpec((cfg.seq, LANE), lambda h, b, s, c: (b, k0 + h)),
            pl.BlockSpec((cfg.seq, LANE), lambda h, b, s, c: (b, v0 + h)),
            pl.BlockSpec((cfg.ctx_len, LANE), lambda h, b, s, c: (ctx_blk0 + b, k0 + h)),
            pl.BlockSpec((cfg.ctx_len, LANE), lambda h, b, s, c: (ctx_blk0 + b, v0 + h)),
            pl.BlockSpec((1, ncls, qb, kwin), lambda h, b, s, c: (h, 0, 0, 0)),
        ],
        out_specs=pl.BlockSpec((cfg.seq, LANE), lambda h, b, s, c: (b, q0 + h)),
    )
    return pl.pallas_call(
        functools.partial(_attn_b_kernel, nblk=nblk, qb=qb, kwin=kwin, w=w),
        grid_spec=grid_spec,
        out_shape=jax.ShapeDtypeStruct(o.shape, o.dtype),
        input_output_aliases={2: 0},
        compiler_params=_params("parallel", "arbitrary"),
        name="attn_neighbourhood",
    )(jnp.asarray(starts, jnp.int32), jnp.asarray(classes, jnp.int32), o, p, p, p, p, p, bias)


def _attn_ctx_kernel(o_in_ref, p_ref, o_ref, *, cfg):
    del o_in_ref
    group = cfg.heads_a // cfg.kv_a

    def cols(off, h):
        return p_ref[:, off + h * LANE: off + (h + 1) * LANE]

    for h in range(cfg.heads_a):
        g = h // group
        o = _softmax_pv([_dot_nt(cols(0, h), cols(cfg.off_ka, g))], [cols(cfg.off_va, g)])
        o_ref[:, h * LANE:(h + 1) * LANE] = o.astype(o_ref.dtype)
    for h in range(cfg.heads_b):
        o = _softmax_pv([_dot_nt(cols(cfg.wa_q, h), cols(cfg.off_kb, h))], [cols(cfg.off_vb, h)])
        o_ref[:, cfg.wa_q + h * LANE: cfg.wa_q + (h + 1) * LANE] = o.astype(o_ref.dtype)


def _attn_ctx(cfg, p, o):
    ctx_blk0 = cfg.n_lat // cfg.ctx_len
    return pl.pallas_call(
        functools.partial(_attn_ctx_kernel, cfg=cfg),
        grid=(cfg.batch,),
        in_specs=[
            pl.BlockSpec(memory_space=pl.ANY),
            pl.BlockSpec((cfg.ctx_len, cfg.in_cols), lambda b: (ctx_blk0 + b, 0)),
        ],
        out_specs=pl.BlockSpec((cfg.ctx_len, cfg.mix_w), lambda b: (ctx_blk0 + b, 0)),
        out_shape=jax.ShapeDtypeStruct(o.shape, o.dtype),
        input_output_aliases={0: 0},
        compiler_params=_params("parallel"),
        name="attn_context",
    )(o, p)


def _out_proj_kernel(o_ref, w_ref, x_ref, gate_ref, y_ref):
    y_ref[...] = x_ref[...] + gate_ref[0, 0] * _dot(o_ref[...], w_ref[...])


def _out_proj(cfg, tl, o, w_out, x_all, mod, n_tiles):
    tm, tn = tl.tm, tl.tn_out
    return pl.pallas_call(
        _out_proj_kernel,
        grid=(n_tiles, cfg.d_model // tn),
        in_specs=[
            pl.BlockSpec((tm, cfg.mix_w), lambda i, j: (i, 0)),
            pl.BlockSpec((cfg.mix_w, tn), lambda i, j: (0, j)),
            pl.BlockSpec((tm, tn), lambda i, j: (i, j)),
            _mod_spec(cfg, tm, 2, tn, 2, col_axis=1),
        ],
        out_specs=pl.BlockSpec((tm, tn), lambda i, j: (i, j)),
        out_shape=jax.ShapeDtypeStruct(x_all.shape, F32),
        compiler_params=_params("parallel", "arbitrary"),
        name="out_proj",
    )(o, w_out, x_all, mod)


def _router_kernel(x_ref, g_ref, shift_ref, scale_ref, wr_ref, gates_ref, *, n_experts):
    h = _modulated(x_ref[...], g_ref, shift_ref, scale_ref)
    logits = jnp.dot(h, wr_ref[...], preferred_element_type=F32, precision=lax.Precision.HIGHEST)
    lane = lax.broadcasted_iota(jnp.int32, logits.shape, 1).astype(F32)
    neg = jnp.float32(-jnp.inf)
    logits = jnp.where(lane < n_experts, logits, neg)
    v1 = jnp.max(logits, axis=-1, keepdims=True)
    i1 = jnp.min(jnp.where(logits == v1, lane, float(LANE)), axis=-1, keepdims=True)
    rest = jnp.where(lane == i1, neg, logits)
    v2 = jnp.max(rest, axis=-1, keepdims=True)
    i2 = jnp.min(jnp.where(rest == v2, lane, float(LANE)), axis=-1, keepdims=True)
    w1 = 1.0 / (1.0 + jnp.exp(v2 - v1))
    gates_ref[...] = jnp.where(lane == i1, w1, 0.0) + jnp.where(lane == i2, 1.0 - w1, 0.0)


def _router(cfg, tl, x_all, g, mod, w_router_pad, n_tiles):
    tm, d = tl.tm, cfg.d_model
    return pl.pallas_call(
        functools.partial(_router_kernel, n_experts=cfg.n_experts),
        grid=(n_tiles, 1),
        in_specs=[
            pl.BlockSpec((tm, d), lambda i, j: (i, 0)),
            pl.BlockSpec((1, d), lambda i, j: (0, 0)),
            _mod_spec(cfg, tm, 3, d, 2),
            _mod_spec(cfg, tm, 4, d, 2),
            pl.BlockSpec((d, LANE), lambda i, j: (0, 0)),
        ],
        out_specs=pl.BlockSpec((tm, LANE), lambda i, j: (i, 0)),
        out_shape=jax.ShapeDtypeStruct((cfg.n_tok, LANE), F32),
        compiler_params=_params("parallel", "arbitrary"),
        name="router",
    )(x_all, g, mod, mod, w_router_pad)


def _ffn_kernel(*refs, gated):
    if gated:
        x_ref, g_ref, shift_ref, scale_ref, gate_ref, w1_ref, w3_ref, w2_ref, eg_ref, y_ref, h_ref = refs
    else:
        x_ref, g_ref, shift_ref, scale_ref, gate_ref, w1_ref, w3_ref, w2_ref, y_ref, h_ref = refs
    e, f = pl.program_id(1), pl.program_id(2)
    first = jnp.logical_and(e == 0, f == 0)
    last = jnp.logical_and(e == pl.num_programs(1) - 1, f == pl.num_programs(2) - 1)

    @pl.when(first)
    def _():
        h_ref[...] = _modulated(x_ref[...], g_ref, shift_ref, scale_ref).astype(BF16)
        y_ref[...] = jnp.zeros_like(y_ref)

    h = h_ref[...]
    a = _dot(h, w1_ref[0])
    b = _dot(h, w3_ref[0])
    u = a * jax.nn.sigmoid(a) * b
    if gated:
        eg = eg_ref[...]
        lane = lax.broadcasted_iota(jnp.int32, eg.shape, 1)
        u = u * jnp.sum(jnp.where(lane == e, eg, 0.0), axis=-1, keepdims=True)
    y_ref[...] += _dot(u.astype(BF16), w2_ref[0])

    @pl.when(last)
    def _():
        y_ref[...] = x_ref[...] + gate_ref[0, 0] * y_ref[...]


def _ffn(cfg, tl, x_all, g, mod, w1, w3, w2, n_tiles, expert_gates=None):
    tm, tf, d = tl.tm, tl.tf, cfg.d_model
    n_e = w1.shape[0]
    gated = expert_gates is not None
    in_specs = [
        pl.BlockSpec((tm, d), lambda i, e, f: (i, 0)),
        pl.BlockSpec((1, d), lambda i, e, f: (0, 0)),
        _mod_spec(cfg, tm, 3, d, 3),
        _mod_spec(cfg, tm, 4, d, 3),
        _mod_spec(cfg, tm, 5, d, 3),
        pl.BlockSpec((1, d, tf), lambda i, e, f: (e, 0, f)),
        pl.BlockSpec((1, d, tf), lambda i, e, f: (e, 0, f)),
        pl.BlockSpec((1, tf, d), lambda i, e, f: (e, f, 0)),
    ]
    args = [x_all, g, mod, mod, mod, w1, w3, w2]
    if gated:
        in_specs.append(pl.BlockSpec((tm, LANE), lambda i, e, f: (i, 0)))
        args.append(expert_gates)
    return pl.pallas_call(
        functools.partial(_ffn_kernel, gated=gated),
        grid=(n_tiles, n_e, cfg.d_ff // tf),
        in_specs=in_specs,
        out_specs=pl.BlockSpec((tm, d), lambda i, e, f: (i, 0)),
        out_shape=jax.ShapeDtypeStruct(x_all.shape, F32),
        scratch_shapes=[pltpu.VMEM((tm, d), BF16)],
        compiler_params=_params("parallel", "arbitrary", "arbitrary"),
        name="ffn_moe" if gated else "ffn_dense",
    )(*args)


def _rope_tables(cfg, tm):
    half = cfg.head_dim // 2
    quarter = half // 2
    t = jnp.arange(cfg.seq)
    freqs = ROPE_THETA ** (-jnp.arange(quarter, dtype=F32) / quarter)
    ang_r = (t // cfg.grid_w).astype(F32)[:, None] * freqs[None, :]
    ang_c = (t % cfg.grid_w).astype(F32)[:, None] * freqs[None, :]
    cos = jnp.concatenate([jnp.cos(ang_r)] * 2 + [jnp.cos(ang_c)] * 2, axis=-1)
    sin = jnp.concatenate([-jnp.sin(ang_r), jnp.sin(ang_r), -jnp.sin(ang_c), jnp.sin(ang_c)], axis=-1)
    cos = jnp.concatenate([cos, jnp.ones((tm, cfg.head_dim), F32)], axis=0)
    sin = jnp.concatenate([sin, jnp.zeros((tm, cfg.head_dim), F32)], axis=0)
    return cos, sin


def _forward(cfg, x, c, ctx, c_ctx, ada_w, ada_b, norm_attn, norm_ffn, w_in, qn_a, kn_a, qn_b, kn_b, rpb,
             w_out, w1_dense, w3_dense, w2_dense, w_router, w1_moe, w3_moe, w2_moe):
    tl = _pick_tiles(cfg)
    d = cfg.d_model
    n_tiles = cfg.n_tok // tl.tm
    n_lat_tiles = cfg.n_lat // tl.tm

    x_all = jnp.concatenate([x.reshape(cfg.n_lat, d), ctx.reshape(cfg.batch * cfg.ctx_len, d)], axis=0)

    cond = jnp.concatenate([c, c_ctx[None, :]], axis=0)
    pad_rows = -(-cond.shape[0] // 16) * 16
    cond_pad = jnp.zeros((pad_rows, d), F32).at[:cond.shape[0]].set(cond)
    mod_all = _ada_mod(cfg, tl, cond_pad, ada_w, ada_b)

    rope_cos, rope_sin = _rope_tables(cfg, tl.tm)
    win, starts, classes, tables = _na_geometry(cfg, tl.rb)
    q_scale = cfg.head_dim ** -0.5
    ones_a = jnp.ones((cfg.wa_kv,), F32)
    ones_b = jnp.ones((cfg.wb,), F32)

    for i in range(cfg.depth):
        last = i == cfg.depth - 1
        j = i // 2
        mod = mod_all[i, :cfg.batch + 1].reshape(cfg.batch + 1, N_MOD, 1, d)
        gain = jnp.concatenate([
            jnp.tile(qn_a[i], cfg.heads_a) * q_scale, jnp.tile(qn_b[i], cfg.heads_b) * q_scale,
            jnp.tile(kn_a[i], cfg.kv_a), ones_a, jnp.tile(kn_b[i], cfg.heads_b), ones_b])[None, :]

        p = _in_proj(cfg, tl, x_all, norm_attn[i][None, :], mod, w_in[i].astype(BF16), gain, rope_cos, rope_sin)
        o = _attn_a(cfg, tl, p)
        o = _attn_b(cfg, tl, p, o, _na_bias(rpb[i], tables), starts, classes, win)
        if not last:
            o = _attn_ctx(cfg, p, o)
        live_tiles = n_lat_tiles if last else n_tiles
        x_all = _out_proj(cfg, tl, o, w_out[i].astype(BF16), x_all, mod, live_tiles)

        g_ffn = norm_ffn[i][None, :]
        if i % 2 == 0:
            x_all = _ffn(cfg, tl, x_all, g_ffn, mod, w1_dense[j][None].astype(BF16),
                         w3_dense[j][None].astype(BF16), w2_dense[j][None].astype(BF16), live_tiles)
        else:
            wr = jnp.zeros((d, LANE), F32).at[:, :cfg.n_experts].set(w_router[j])
            gates = _router(cfg, tl, x_all, g_ffn, mod, wr, live_tiles)
            x_all = _ffn(cfg, tl, x_all, g_ffn, mod, w1_moe[j].astype(BF16), w3_moe[j].astype(BF16),
                         w2_moe[j].astype(BF16), live_tiles, expert_gates=gates)

    return x_all[:cfg.n_lat].reshape(cfg.batch, cfg.seq, d)


def kernel(x, c, ctx, c_ctx, ada_w, ada_b, norm_attn, norm_ffn, w_in, qn_a, kn_a, qn_b, kn_b, rpb, w_out,
           w1_dense, w3_dense, w2_dense, w_router, w1_moe, w3_moe, w2_moe):
    batch, seq, d_model = x.shape
    depth = w_in.shape[0]
    head_dim = qn_a.shape[-1]
    heads_b = rpb.shape[1]
    wb = heads_b * head_dim
    wa_q = w_out.shape[1] - wb
    wa_kv = (w_in.shape[2] - wa_q - 3 * wb) // 2
    cfg = Cfg(d_model=d_model, batch=batch, seq=seq, ctx_len=ctx.shape[1], grid_w=64, head_dim=head_dim,
              heads_a=wa_q // head_dim, kv_a=wa_kv // head_dim, heads_b=heads_b,
              na_kh=(rpb.shape[2] + 1) // 2, na_kw=(rpb.shape[3] + 1) // 2, d_ff=w1_dense.shape[-1],
              n_experts=w_router.shape[-1], depth=depth)
    return _forward(cfg, x, c, ctx, c_ctx, ada_w, ada_b, norm_attn, norm_ffn, w_in, qn_a, kn_a, qn_b, kn_b,
                    rpb, w_out, w1_dense, w3_dense, w2_dense, w_router, w1_moe, w3_moe, w2_moe)
```

```python
import functools
from typing import NamedTuple

import numpy as np
import jax
import jax.numpy as jnp
from jax import lax
from jax.experimental import pallas as pl
from jax.experimental.pallas import tpu as pltpu

F32 = jnp.float32
BF16 = jnp.bfloat16

EPS = 1e-6
ROPE_THETA = 10000.0
N_MOD = 6
LANE = 128
MASK_VALUE = -1e30
V7X_VMEM_LIMIT = 56 * 1024 * 1024


class Cfg(NamedTuple):
    d_model: int
    batch: int
    seq: int
    ctx_len: int
    grid_w: int
    head_dim: int
    heads_a: int
    kv_a: int
    heads_b: int
    na_kh: int
    na_kw: int
    d_ff: int
    n_experts: int
    depth: int

    @property
    def wa_q(self):
        return self.heads_a * self.head_dim

    @property
    def wa_kv(self):
        return self.kv_a * self.head_dim

    @property
    def wb(self):
        return self.heads_b * self.head_dim

    @property
    def mix_w(self):
        return self.wa_q + self.wb

    @property
    def in_cols(self):
        return self.mix_w + 2 * self.wa_kv + 2 * self.wb

    @property
    def rows(self):
        return self.seq // self.grid_w

    @property
    def n_lat(self):
        return self.batch * self.seq

    @property
    def n_tok(self):
        return self.batch * (self.seq + self.ctx_len)

    @property
    def off_ka(self):
        return self.mix_w

    @property
    def off_va(self):
        return self.mix_w + self.wa_kv

    @property
    def off_kb(self):
        return self.mix_w + 2 * self.wa_kv

    @property
    def off_vb(self):
        return self.mix_w + 2 * self.wa_kv + self.wb


class Tiles(NamedTuple):
    tm: int
    tn_proj: int
    tn_out: int
    tf: int
    tq: int
    rb: int
    tn_ada: int


def _pick_tiles(cfg):
    n_ctx = cfg.batch * cfg.ctx_len
    tm = min(512, cfg.seq, n_ctx)
    assert cfg.seq % tm == 0 and n_ctx % tm == 0
    tn_proj = 2 * cfg.head_dim
    for w in (cfg.wa_q, cfg.wb, cfg.wa_kv):
        assert w % tn_proj == 0
    tn_out = min(512, cfg.d_model)
    tf = min(512, cfg.d_ff)
    assert cfg.d_model % tn_out == 0 and cfg.d_ff % tf == 0
    tq = min(256, cfg.seq)
    assert cfg.seq % tq == 0
    rb = 4
    assert cfg.rows % rb == 0 and cfg.rows >= rb + cfg.na_kh
    tn_ada = next(t for t in (1024, 512, 256, LANE) if (N_MOD * cfg.d_model) % t == 0)
    return Tiles(tm, tn_proj, tn_out, tf, tq, rb, tn_ada)


def _params(*sem):
    return pltpu.CompilerParams(dimension_semantics=sem, vmem_limit_bytes=V7X_VMEM_LIMIT)


def _dot(a, b):
    return jnp.dot(a, b, preferred_element_type=F32)


def _dot_nt(a, b):
    return lax.dot_general(a, b, (((1,), (1,)), ((), ())), preferred_element_type=F32)


def _rms(x):
    return x * lax.rsqrt(jnp.mean(x * x, axis=-1, keepdims=True) + EPS)


def _ada_kernel(cond_ref, w_ref, b_ref, o_ref):
    c = cond_ref[...]
    c = (c * jax.nn.sigmoid(c)).astype(BF16)
    o_ref[0] = _dot(c, w_ref[0].astype(BF16)) + b_ref[0]


def _ada_mod(cfg, tl, cond_pad, ada_w, ada_b):
    depth, d, n6 = ada_w.shape
    mp = cond_pad.shape[0]
    return pl.pallas_call(
        _ada_kernel,
        grid=(depth, n6 // tl.tn_ada),
        in_specs=[
            pl.BlockSpec((mp, d), lambda l, j: (0, 0)),
            pl.BlockSpec((1, d, tl.tn_ada), lambda l, j: (l, 0, j)),
            pl.BlockSpec((1, 1, tl.tn_ada), lambda l, j: (l, 0, j)),
        ],
        out_specs=pl.BlockSpec((1, mp, tl.tn_ada), lambda l, j: (l, 0, j)),
        out_shape=jax.ShapeDtypeStruct((depth, mp, n6), F32),
        compiler_params=_params("parallel", "parallel"),
        name="ada_mod",
    )(cond_pad, ada_w, ada_b.reshape(depth, 1, n6))


def _mod_index(cfg, tm):
    n_lat_tiles = cfg.n_lat // tm
    per_sample = cfg.seq // tm

    def f(i):
        return jnp.where(i < n_lat_tiles, i // per_sample, cfg.batch)
    return f


def _mod_spec(cfg, tm, k, ncols, grid_rank, col_axis=None):
    mi = _mod_index(cfg, tm)
    if col_axis is None:
        return pl.BlockSpec((1, 1, 1, ncols), lambda i, *rest: (mi(i), k, 0, 0))
    assert grid_rank == 2 and col_axis == 1
    return pl.BlockSpec((1, 1, 1, ncols), lambda i, j: (mi(i), k, 0, j))


def _modulated(x, g_ref, shift_ref, scale_ref):
    y = _rms(x) * g_ref[...]
    return y * (1.0 + scale_ref[0, 0]) + shift_ref[0, 0]


def _swap_halves(y):
    lane = lax.broadcasted_iota(jnp.int32, y.shape, 1)
    quarter = LANE // 4
    fwd = pltpu.roll(y, LANE - quarter, 1)
    bwd = pltpu.roll(y, quarter, 1)
    return jnp.where((lane % (2 * quarter)) < quarter, fwd, bwd)


def _proj_kernel(x_ref, g_ref, shift_ref, scale_ref, w_ref, gain_ref, cos_ref, sin_ref, o_ref, h_ref,
                 *, rope_tiles, norm_tiles, heads_per_tile):
    j = pl.program_id(1)

    @pl.when(j == 0)
    def _():
        h_ref[...] = _modulated(x_ref[...], g_ref, shift_ref, scale_ref).astype(BF16)

    acc = _dot(h_ref[...], w_ref[...])

    def in_ranges(ranges):
        cond = None
        for lo, hi in ranges:
            c = jnp.logical_and(j >= lo, j < hi)
            cond = c if cond is None else jnp.logical_or(cond, c)
        return cond

    is_rope = in_ranges(rope_tiles)
    is_norm = in_ranges(norm_tiles)

    def head(c):
        return acc[:, c * LANE:(c + 1) * LANE]

    @pl.when(is_rope)
    def _():
        for c in range(heads_per_tile):
            y = _rms(head(c)) * gain_ref[:, c * LANE:(c + 1) * LANE]
            y = y * cos_ref[...] + _swap_halves(y) * sin_ref[...]
            o_ref[:, c * LANE:(c + 1) * LANE] = y.astype(o_ref.dtype)

    @pl.when(is_norm)
    def _():
        for c in range(heads_per_tile):
            y = _rms(head(c)) * gain_ref[:, c * LANE:(c + 1) * LANE]
            o_ref[:, c * LANE:(c + 1) * LANE] = y.astype(o_ref.dtype)

    @pl.when(jnp.logical_not(jnp.logical_or(is_rope, is_norm)))
    def _():
        o_ref[...] = acc.astype(o_ref.dtype)


def _in_proj(cfg, tl, x_all, g, mod, w_in, gain, rope_cos, rope_sin):
    tm, tn = tl.tm, tl.tn_proj
    d = cfg.d_model
    n_tiles = cfg.n_tok // tm
    n_lat_tiles = cfg.n_lat // tm
    per_sample = cfg.seq // tm
    u = lambda cols: cols // tn
    qa = (0, u(cfg.wa_q))
    qb = (qa[1], qa[1] + u(cfg.wb))
    ka = (qb[1], qb[1] + u(cfg.wa_kv))
    va = (ka[1], ka[1] + u(cfg.wa_kv))
    kb = (va[1], va[1] + u(cfg.wb))
    kern = functools.partial(_proj_kernel, rope_tiles=(qa, ka), norm_tiles=(qb, kb),
                             heads_per_tile=tn // LANE)

    def rope_idx(i, j):
        return (jnp.where(i < n_lat_tiles, i % per_sample, per_sample), 0)

    return pl.pallas_call(
        kern,
        grid=(n_tiles, cfg.in_cols // tn),
        in_specs=[
            pl.BlockSpec((tm, d), lambda i, j: (i, 0)),
            pl.BlockSpec((1, d), lambda i, j: (0, 0)),
            _mod_spec(cfg, tm, 0, d, 2),
            _mod_spec(cfg, tm, 1, d, 2),
            pl.BlockSpec((d, tn), lambda i, j: (0, j)),
            pl.BlockSpec((1, tn), lambda i, j: (0, j)),
            pl.BlockSpec((tm, LANE), rope_idx),
            pl.BlockSpec((tm, LANE), rope_idx),
        ],
        out_specs=pl.BlockSpec((tm, tn), lambda i, j: (i, j)),
        out_shape=jax.ShapeDtypeStruct((cfg.n_tok, cfg.in_cols), BF16),
        scratch_shapes=[pltpu.VMEM((tm, d), BF16)],
        compiler_params=_params("parallel", "arbitrary"),
        name="in_proj",
    )(x_all, g, mod, mod, w_in, gain, rope_cos, rope_sin)


def _softmax_pv(s_parts, v_parts):
    m = None
    for s in s_parts:
        mx = jnp.max(s, axis=-1, keepdims=True)
        m = mx if m is None else jnp.maximum(m, mx)
    l = None
    o = None
    for s, v in zip(s_parts, v_parts):
        p = jnp.exp(s - m)
        ps = jnp.sum(p, axis=-1, keepdims=True)
        pv = _dot(p.astype(BF16), v)
        l = ps if l is None else l + ps
        o = pv if o is None else o + pv
    return o / l


def _attn_a_kernel(q_ref, kl_ref, vl_ref, kc_ref, vc_ref, o_ref, *, group):
    kl, vl, kc, vc = kl_ref[...], vl_ref[...], kc_ref[...], vc_ref[...]
    for h in range(group):
        q = q_ref[:, h * LANE:(h + 1) * LANE]
        o = _softmax_pv([_dot_nt(q, kl), _dot_nt(q, kc)], [vl, vc])
        o_ref[:, h * LANE:(h + 1) * LANE] = o.astype(o_ref.dtype)


def _attn_a(cfg, tl, p):
    assert cfg.head_dim == LANE
    group = cfg.heads_a // cfg.kv_a
    gw = group * LANE
    tq = tl.tq
    nq = cfg.seq // tq
    ctx_blk0 = cfg.n_lat // cfg.ctx_len
    ka0, va0 = cfg.off_ka // LANE, cfg.off_va // LANE
    return pl.pallas_call(
        functools.partial(_attn_a_kernel, group=group),
        grid=(cfg.batch, cfg.kv_a, nq),
        in_specs=[
            pl.BlockSpec((tq, gw), lambda b, g, i: (b * nq + i, g)),
            pl.BlockSpec((cfg.seq, LANE), lambda b, g, i: (b, ka0 + g)),
            pl.BlockSpec((cfg.seq, LANE), lambda b, g, i: (b, va0 + g)),
            pl.BlockSpec((cfg.ctx_len, LANE), lambda b, g, i: (ctx_blk0 + b, ka0 + g)),
            pl.BlockSpec((cfg.ctx_len, LANE), lambda b, g, i: (ctx_blk0 + b, va0 + g)),
        ],
        out_specs=pl.BlockSpec((tq, gw), lambda b, g, i: (b * nq + i, g)),
        out_shape=jax.ShapeDtypeStruct((cfg.n_tok, cfg.mix_w), BF16),
        compiler_params=_params("parallel", "parallel", "arbitrary"),
        name="attn_global",
    )(p, p, p, p, p)


def _na_geometry(cfg, rb):
    rows, w = cfg.rows, cfg.grid_w
    kh = min(cfg.na_kh, rows)
    kw = cfg.na_kw
    win = min(rows, -(-(rb - 1 + kh) // rb) * rb)
    nblk = rows // rb
    r0 = np.clip(np.arange(rows) - kh // 2, 0, rows - kh)
    c0 = np.clip(np.arange(w) - kw // 2, 0, w - kw)
    starts = np.clip(rb * np.arange(nblk) - kh // 2, 0, rows - win)
    col = np.arange(w)
    vc = (col[None, :] >= c0[:, None]) & (col[None, :] < c0[:, None] + kw)
    dc = np.where(vc, col[None, :] - col[:, None] + (cfg.na_kw - 1), 0)

    def row_table(i):
        r = rb * i + np.arange(rb)
        key_r = starts[i] + np.arange(win)
        vr = (key_r[None, :] >= r0[r][:, None]) & (key_r[None, :] < r0[r][:, None] + kh)
        assert (vr.sum(axis=1) == kh).all()
        return vr, np.where(vr, key_r[None, :] - r[:, None] + (cfg.na_kh - 1), 0)

    row_tables, classes = [], []
    for i in range(nblk):
        t = row_table(i)
        for c, u in enumerate(row_tables):
            if all(np.array_equal(a, b) for a, b in zip(t, u)):
                classes.append(c)
                break
        else:
            classes.append(len(row_tables))
            row_tables.append(t)
    return win, starts, np.asarray(classes), row_tables, (vc, dc)


def _na_bias(rpb_l, row_tables, col_table):
    h, n_dr, n_dc = rpb_l.shape
    vc, dc = col_table
    w = vc.shape[0]
    onehot = (dc[None] == np.arange(n_dc)[:, None, None]).astype(np.float32)
    t = jnp.einsum("hdk,kqc->hdqc", rpb_l, onehot, precision=lax.Precision.HIGHEST)
    out = []
    for vr, dr in row_tables:
        rb, win = vr.shape
        b = jnp.take(t, dr.reshape(-1), axis=1).reshape(h, rb, win, w, w)
        valid = vr[:, :, None, None] & vc[None, None]
        b = jnp.where(valid[None], b, MASK_VALUE).transpose(0, 1, 3, 2, 4)
        out.append(b.reshape(h, rb * w, win * w))
    return jnp.stack(out, axis=1).astype(F32)


def _attn_b_kernel(start_ref, cls_ref, o_in_ref, q_ref, kl_ref, vl_ref, kc_ref, vc_ref, bias_ref, o_ref,
                   *, nblk, qb, kwin, w):
    del o_in_ref
    kc, vc = kc_ref[...], vc_ref[...]

    def body(i, carry):
        k0 = pl.multiple_of(start_ref[i] * w, w)
        q = q_ref[pl.ds(pl.multiple_of(i * qb, qb), qb), :]
        kw_ = kl_ref[pl.ds(k0, kwin), :]
        vw = vl_ref[pl.ds(k0, kwin), :]
        s_loc = _dot_nt(q, kw_) + bias_ref[0, cls_ref[i]]
        o = _softmax_pv([s_loc, _dot_nt(q, kc)], [vw, vc])
        o_ref[pl.ds(pl.multiple_of(i * qb, qb), qb), :] = o.astype(o_ref.dtype)
        return carry

    lax.fori_loop(0, nblk, body, 0)


def _attn_b(cfg, tl, p, o, bias, starts, classes, win):
    w = cfg.grid_w
    qb, kwin = tl.rb * w, win * w
    nblk = cfg.rows // tl.rb
    ncls = bias.shape[1]
    ctx_blk0 = cfg.n_lat // cfg.ctx_len
    q0, k0, v0 = cfg.wa_q // LANE, cfg.off_kb // LANE, cfg.off_vb // LANE
    grid_spec = pltpu.PrefetchScalarGridSpec(
        num_scalar_prefetch=2,
        grid=(cfg.heads_b, cfg.batch),
        in_specs=[
            pl.BlockSpec(memory_space=pl.ANY),
            pl.BlockSpec((cfg.seq, LANE), lambda h, b, s, c: (b, q0 + h)),
            pl.BlockSpec((cfg.seq, LANE), lambda h, b, s, c: (b, k0 + h)),
            pl.BlockSpec((cfg.seq, LANE), lambda h, b, s, c: (b, v0 + h)),
            pl.BlockSpec((cfg.ctx_len, LANE), lambda h, b, s, c: (ctx_blk0 + b, k0 + h)),
            pl.BlockSpec((cfg.ctx_len, LANE), lambda h, b, s, c: (ctx_blk0 + b, v0 + h)),
            pl.BlockSpec((1, ncls, qb, kwin), lambda h, b, s, c: (h, 0, 0, 0)),
        ],
        out_specs=pl.BlockSpec((cfg.seq, LANE), lambda h, b, s, c: (b, q0 + h)),
    )
    return pl.pallas_call(
        functools.partial(_attn_b_kernel, nblk=nblk, qb=qb, kwin=kwin, w=w),
        grid_spec=grid_spec,
        out_shape=jax.ShapeDtypeStruct(o.shape, o.dtype),
        input_output_aliases={2: 0},
        compiler_params=_params("parallel", "arbitrary"),
        name="attn_neighbourhood",
    )(jnp.asarray(starts, jnp.int32), jnp.asarray(classes, jnp.int32), o, p, p, p, p, p, bias)


def _attn_ctx_kernel(o_in_ref, p_ref, o_ref, *, cfg):
    del o_in_ref
    group = cfg.heads_a // cfg.kv_a

    def cols(off, h):
        return p_ref[:, off + h * LANE: off + (h + 1) * LANE]

    for h in range(cfg.heads_a):
        g = h // group
        o = _softmax_pv([_dot_nt(cols(0, h), cols(cfg.off_ka, g))], [cols(cfg.off_va, g)])
        o_ref[:, h * LANE:(h + 1) * LANE] = o.astype(o_ref.dtype)
    for h in range(cfg.heads_b):
        o = _softmax_pv([_dot_nt(cols(cfg.wa_q, h), cols(cfg.off_kb, h))], [cols(cfg.off_vb, h)])
        o_ref[:, cfg.wa_q + h * LANE: cfg.wa_q + (h + 1) * LANE] = o.astype(o_ref.dtype)


def _attn_ctx(cfg, p, o):
    ctx_blk0 = cfg.n_lat // cfg.ctx_len
    return pl.pallas_call(
        functools.partial(_attn_ctx_kernel, cfg=cfg),
        grid=(cfg.batch,),
        in_specs=[
            pl.BlockSpec(memory_space=pl.ANY),
            pl.BlockSpec((cfg.ctx_len, cfg.in_cols), lambda b: (ctx_blk0 + b, 0)),
        ],
        out_specs=pl.BlockSpec((cfg.ctx_len, cfg.mix_w), lambda b: (ctx_blk0 + b, 0)),
        out_shape=jax.ShapeDtypeStruct(o.shape, o.dtype),
        input_output_aliases={0: 0},
        compiler_params=_params("parallel"),
        name="attn_context",
    )(o, p)


def _out_proj_kernel(o_ref, w_ref, x_ref, gate_ref, y_ref):
    y_ref[...] = x_ref[...] + gate_ref[0, 0] * _dot(o_ref[...], w_ref[...])


def _out_proj(cfg, tl, o, w_out, x_all, mod, n_tiles):
    tm, tn = tl.tm, tl.tn_out
    return pl.pallas_call(
        _out_proj_kernel,
        grid=(n_tiles, cfg.d_model // tn),
        in_specs=[
            pl.BlockSpec((tm, cfg.mix_w), lambda i, j: (i, 0)),
            pl.BlockSpec((cfg.mix_w, tn), lambda i, j: (0, j)),
            pl.BlockSpec((tm, tn), lambda i, j: (i, j)),
            _mod_spec(cfg, tm, 2, tn, 2, col_axis=1),
        ],
        out_specs=pl.BlockSpec((tm, tn), lambda i, j: (i, j)),
        out_shape=jax.ShapeDtypeStruct(x_all.shape, F32),
        compiler_params=_params("parallel", "arbitrary"),
        name="out_proj",
    )(o, w_out, x_all, mod)


def _router_kernel(x_ref, g_ref, shift_ref, scale_ref, wr_ref, h_ref, route_ref, *, n_experts):
    h = _modulated(x_ref[...], g_ref, shift_ref, scale_ref)
    h_ref[...] = h
    logits = jnp.dot(h, wr_ref[...], preferred_element_type=F32, precision=lax.Precision.HIGHEST)
    lane = lax.broadcasted_iota(jnp.int32, logits.shape, 1).astype(F32)
    neg = jnp.float32(-jnp.inf)
    logits = jnp.where(lane < n_experts, logits, neg)
    v1 = jnp.max(logits, axis=-1, keepdims=True)
    i1 = jnp.min(jnp.where(logits == v1, lane, float(LANE)), axis=-1, keepdims=True)
    rest = jnp.where(lane == i1, neg, logits)
    v2 = jnp.max(rest, axis=-1, keepdims=True)
    i2 = jnp.min(jnp.where(rest == v2, lane, float(LANE)), axis=-1, keepdims=True)
    w1 = 1.0 / (1.0 + jnp.exp(v2 - v1))
    route = jnp.where(lane == 0.0, w1, jnp.where(lane == 1.0, 1.0 - w1, jnp.where(lane == 2.0, i1, i2)))
    route_ref[...] = jnp.where(lane < 4.0, route, 0.0)


def _router(cfg, tl, x_all, g, mod, w_router_pad, n_tiles):
    tm, d = tl.tm, cfg.d_model
    return pl.pallas_call(
        functools.partial(_router_kernel, n_experts=cfg.n_experts),
        grid=(n_tiles,),
        in_specs=[
            pl.BlockSpec((tm, d), lambda i: (i, 0)),
            pl.BlockSpec((1, d), lambda i: (0, 0)),
            _mod_spec(cfg, tm, 3, d, 1),
            _mod_spec(cfg, tm, 4, d, 1),
            pl.BlockSpec((d, LANE), lambda i: (0, 0)),
        ],
        out_specs=[pl.BlockSpec((tm, d), lambda i: (i, 0)), pl.BlockSpec((tm, LANE), lambda i: (i, 0))],
        out_shape=[jax.ShapeDtypeStruct((n_tiles * tm, d), F32), jax.ShapeDtypeStruct((n_tiles * tm, LANE), F32)],
        compiler_params=_params("parallel"),
        name="router",
    )(x_all, g, mod, mod, w_router_pad)


def _swiglu_step(h, w1_ref, w3_ref, w2_ref):
    a = _dot(h, w1_ref[0])
    b = _dot(h, w3_ref[0])
    return _dot((a * jax.nn.sigmoid(a) * b).astype(BF16), w2_ref[0])


def _ffn_kernel(x_ref, g_ref, shift_ref, scale_ref, gate_ref, w1_ref, w3_ref, w2_ref, y_ref, h_ref):
    f = pl.program_id(1)

    @pl.when(f == 0)
    def _():
        h_ref[...] = _modulated(x_ref[...], g_ref, shift_ref, scale_ref).astype(BF16)
        y_ref[...] = jnp.zeros_like(y_ref)

    y_ref[...] += _swiglu_step(h_ref[...], w1_ref, w3_ref, w2_ref)

    @pl.when(f == pl.num_programs(1) - 1)
    def _():
        y_ref[...] = x_ref[...] + gate_ref[0, 0] * y_ref[...]


def _ffn(cfg, tl, x_all, g, mod, w1, w3, w2, n_tiles):
    tm, tf, d = tl.tm, tl.tf, cfg.d_model
    return pl.pallas_call(
        _ffn_kernel,
        grid=(n_tiles, cfg.d_ff // tf),
        in_specs=[
            pl.BlockSpec((tm, d), lambda i, f: (i, 0)),
            pl.BlockSpec((1, d), lambda i, f: (0, 0)),
            _mod_spec(cfg, tm, 3, d, 2),
            _mod_spec(cfg, tm, 4, d, 2),
            _mod_spec(cfg, tm, 5, d, 2),
            pl.BlockSpec((1, d, tf), lambda i, f: (0, 0, f)),
            pl.BlockSpec((1, d, tf), lambda i, f: (0, 0, f)),
            pl.BlockSpec((1, tf, d), lambda i, f: (0, f, 0)),
        ],
        out_specs=pl.BlockSpec((tm, d), lambda i, f: (i, 0)),
        out_shape=jax.ShapeDtypeStruct(x_all.shape, F32),
        scratch_shapes=[pltpu.VMEM((tm, d), BF16)],
        compiler_params=_params("parallel", "arbitrary"),
        name="ffn_dense",
    )(x_all, g, mod, mod, mod, w1, w3, w2)


def _route_plan(cfg, tm, route, n_rows):
    n_e = cfg.n_experts
    n_assign = 2 * n_rows
    n_tiles_max = n_assign // tm + n_e
    expert = route[:, 2:4].astype(jnp.int32).reshape(n_assign)
    onehot = (expert[:, None] == jnp.arange(n_e)[None, :]).astype(jnp.int32)
    csum = jnp.cumsum(onehot, axis=0)
    rank = jnp.sum(csum * onehot, axis=1) - 1
    counts = csum[-1]
    tiles_per_e = (counts + tm - 1) // tm
    tile_end = jnp.cumsum(tiles_per_e)
    tile_start = tile_end - tiles_per_e
    n_used = tile_end[-1]
    pos = jnp.sum(onehot * tile_start[None, :], axis=1) * tm + rank
    tile_ids = jnp.arange(n_tiles_max)
    tile_expert = jnp.minimum(jnp.sum(tile_ids[:, None] >= tile_end[None, :], axis=1), n_e - 1)
    tile_expert = jnp.where(tile_ids < n_used, tile_expert, tile_expert[jnp.maximum(n_used - 1, 0)])
    order = jnp.argsort(expert, stable=True)
    unpadded_start = jnp.cumsum(counts) - counts
    k = (tile_ids - tile_start[tile_expert])[:, None] * tm + jnp.arange(tm)[None, :]
    valid = jnp.logical_and(k < counts[tile_expert][:, None], (tile_ids < n_used)[:, None])
    src = order[jnp.clip(unpadded_start[tile_expert][:, None] + k, 0, n_assign - 1)] // 2
    src_tok = jnp.where(valid, src, 0).astype(jnp.int32).reshape(n_tiles_max, 1, tm)
    pos = pos.astype(jnp.int32).reshape(n_rows // tm, 1, tm, 2)
    return (tile_expert.astype(jnp.int32), n_used.astype(jnp.int32).reshape(1), src_tok,
            pos[..., 0], pos[..., 1])


def _start_row_gather(idx_ref, src_hbm, dst_ref, sem, n_rows):
    def body(r, carry):
        pltpu.make_async_copy(src_hbm.at[pl.ds(idx_ref[0, 0, r], 1), :], dst_ref.at[pl.ds(r, 1), :], sem).start()
        return carry
    lax.fori_loop(0, n_rows, body, 0, unroll=8)


def _wait_row_gather(src_hbm, dst_ref, sem, n_rows):
    def body(r, carry):
        pltpu.make_async_copy(src_hbm.at[pl.ds(0, 1), :], dst_ref.at[pl.ds(r, 1), :], sem).wait()
        return carry
    lax.fori_loop(0, n_rows, body, 0, unroll=8)


def _moe_ffn_kernel(te_ref, nu_ref, src0_ref, src1_ref, h_hbm, w1_ref, w3_ref, w2_ref, y_ref,
                    hbuf, h_ref, sem, *, tm):
    del te_ref
    t, f = pl.program_id(0), pl.program_id(1)
    n_used = nu_ref[0]
    slot = lax.rem(t, 2)
    live = t < n_used
    at_tile_start = f == 0

    @pl.when(jnp.logical_and(at_tile_start, t == 0))
    def _():
        _start_row_gather(src0_ref, h_hbm, hbuf.at[0], sem.at[0], tm)

    @pl.when(jnp.logical_and(at_tile_start, t + 1 < n_used))
    def _():
        _start_row_gather(src1_ref, h_hbm, hbuf.at[1 - slot], sem.at[1 - slot], tm)

    @pl.when(jnp.logical_and(at_tile_start, live))
    def _():
        _wait_row_gather(h_hbm, hbuf.at[slot], sem.at[slot], tm)
        h_ref[...] = hbuf[slot].astype(BF16)
        y_ref[...] = jnp.zeros_like(y_ref)

    @pl.when(live)
    def _():
        y_ref[...] += _swiglu_step(h_ref[...], w1_ref, w3_ref, w2_ref)


def _moe_ffn(cfg, tl, h, w1, w3, w2, tile_expert, n_used, src_tok):
    tm, tf, d = tl.tm, tl.tf, cfg.d_model
    n_tiles_max = src_tok.shape[0]
    nf = cfg.d_ff // tf

    def live_tile(t, nu):
        return jnp.minimum(t, nu[0] - 1)

    def f_idx(t, f, nu):
        return jnp.where(t < nu[0], f, nf - 1)

    grid_spec = pltpu.PrefetchScalarGridSpec(
        num_scalar_prefetch=2,
        grid=(n_tiles_max, nf),
        in_specs=[
            pl.BlockSpec((1, 1, tm), lambda t, f, te, nu: (0, 0, 0), memory_space=pltpu.SMEM),
            pl.BlockSpec((1, 1, tm), lambda t, f, te, nu: (live_tile(t + 1, nu), 0, 0), memory_space=pltpu.SMEM),
            pl.BlockSpec(memory_space=pl.ANY),
            pl.BlockSpec((1, d, tf), lambda t, f, te, nu: (te[t], 0, f_idx(t, f, nu))),
            pl.BlockSpec((1, d, tf), lambda t, f, te, nu: (te[t], 0, f_idx(t, f, nu))),
            pl.BlockSpec((1, tf, d), lambda t, f, te, nu: (te[t], f_idx(t, f, nu), 0)),
        ],
        out_specs=pl.BlockSpec((tm, d), lambda t, f, te, nu: (live_tile(t, nu), 0)),
        scratch_shapes=[pltpu.VMEM((2, tm, d), F32), pltpu.VMEM((tm, d), BF16), pltpu.SemaphoreType.DMA((2,))],
    )
    return pl.pallas_call(
        functools.partial(_moe_ffn_kernel, tm=tm),
        grid_spec=grid_spec,
        out_shape=jax.ShapeDtypeStruct((n_tiles_max * tm, d), F32),
        compiler_params=_params("arbitrary", "arbitrary"),
        name="moe_ffn",
    )(tile_expert, n_used, src_tok, src_tok, h, w1, w3, w2)


def _moe_combine_kernel(pa0_ref, pb0_ref, pa1_ref, pb1_ref, y_hbm, x_ref, gate_ref, route_ref, o_ref,
                        ya, yb, sem, *, tm):
    i = pl.program_id(0)
    slot = lax.rem(i, 2)

    def start(pa_ref, pb_ref, s):
        _start_row_gather(pa_ref, y_hbm, ya.at[s], sem.at[s], tm)
        _start_row_gather(pb_ref, y_hbm, yb.at[s], sem.at[s], tm)

    @pl.when(i == 0)
    def _():
        start(pa0_ref, pb0_ref, 0)

    @pl.when(i + 1 < pl.num_programs(0))
    def _():
        start(pa1_ref, pb1_ref, 1 - slot)

    _wait_row_gather(y_hbm, ya.at[slot], sem.at[slot], tm)
    _wait_row_gather(y_hbm, yb.at[slot], sem.at[slot], tm)
    route = route_ref[...]
    y = route[:, 0:1] * ya[slot] + route[:, 1:2] * yb[slot]
    o_ref[...] = x_ref[...] + gate_ref[0, 0] * y


def _moe_combine(cfg, tl, y_sorted, x_all, mod, route, pos_a, pos_b, n_tiles):
    tm, d = tl.tm, cfg.d_model
    first = pl.BlockSpec((1, 1, tm), lambda i: (0, 0, 0), memory_space=pltpu.SMEM)
    nxt = pl.BlockSpec((1, 1, tm), lambda i: (jnp.minimum(i + 1, n_tiles - 1), 0, 0), memory_space=pltpu.SMEM)
    return pl.pallas_call(
        functools.partial(_moe_combine_kernel, tm=tm),
        grid=(n_tiles,),
        in_specs=[
            first, first, nxt, nxt,
            pl.BlockSpec(memory_space=pl.ANY),
            pl.BlockSpec((tm, d), lambda i: (i, 0)),
            _mod_spec(cfg, tm, 5, d, 1),
            pl.BlockSpec((tm, LANE), lambda i: (i, 0)),
        ],
        out_specs=pl.BlockSpec((tm, d), lambda i: (i, 0)),
        out_shape=jax.ShapeDtypeStruct(x_all.shape, F32),
        scratch_shapes=[pltpu.VMEM((2, tm, d), F32), pltpu.VMEM((2, tm, d), F32), pltpu.SemaphoreType.DMA((2,))],
        compiler_params=_params("arbitrary"),
        name="moe_combine",
    )(pos_a, pos_b, pos_a, pos_b, y_sorted, x_all, mod, route)


def _rope_tables(cfg, tm):
    half = cfg.head_dim // 2
    quarter = half // 2
    t = jnp.arange(cfg.seq)
    freqs = ROPE_THETA ** (-jnp.arange(quarter, dtype=F32) / quarter)
    ang_r = (t // cfg.grid_w).astype(F32)[:, None] * freqs[None, :]
    ang_c = (t % cfg.grid_w).astype(F32)[:, None] * freqs[None, :]
    cos = jnp.concatenate([jnp.cos(ang_r)] * 2 + [jnp.cos(ang_c)] * 2, axis=-1)
    sin = jnp.concatenate([-jnp.sin(ang_r), jnp.sin(ang_r), -jnp.sin(ang_c), jnp.sin(ang_c)], axis=-1)
    cos = jnp.concatenate([cos, jnp.ones((tm, cfg.head_dim), F32)], axis=0)
    sin = jnp.concatenate([sin, jnp.zeros((tm, cfg.head_dim), F32)], axis=0)
    return cos, sin


def _forward(cfg, x, c, ctx, c_ctx, ada_w, ada_b, norm_attn, norm_ffn, w_in, qn_a, kn_a, qn_b, kn_b, rpb,
             w_out, w1_dense, w3_dense, w2_dense, w_router, w1_moe, w3_moe, w2_moe):
    tl = _pick_tiles(cfg)
    d = cfg.d_model
    n_tiles = cfg.n_tok // tl.tm
    n_lat_tiles = cfg.n_lat // tl.tm

    x_all = jnp.concatenate([x.reshape(cfg.n_lat, d), ctx.reshape(cfg.batch * cfg.ctx_len, d)], axis=0)

    cond = jnp.concatenate([c, c_ctx[None, :]], axis=0)
    pad_rows = -(-cond.shape[0] // 16) * 16
    cond_pad = jnp.zeros((pad_rows, d), F32).at[:cond.shape[0]].set(cond)
    mod_all = _ada_mod(cfg, tl, cond_pad, ada_w, ada_b)

    rope_cos, rope_sin = _rope_tables(cfg, tl.tm)
    win, starts, classes, row_tables, col_table = _na_geometry(cfg, tl.rb)
    q_scale = cfg.head_dim ** -0.5
    ones_a = jnp.ones((cfg.wa_kv,), F32)
    ones_b = jnp.ones((cfg.wb,), F32)

    for i in range(cfg.depth):
        last = i == cfg.depth - 1
        j = i // 2
        mod = mod_all[i, :cfg.batch + 1].reshape(cfg.batch + 1, N_MOD, 1, d)
        gain = jnp.concatenate([
            jnp.tile(qn_a[i], cfg.heads_a) * q_scale, jnp.tile(qn_b[i], cfg.heads_b) * q_scale,
            jnp.tile(kn_a[i], cfg.kv_a), ones_a, jnp.tile(kn_b[i], cfg.heads_b), ones_b])[None, :]

        p = _in_proj(cfg, tl, x_all, norm_attn[i][None, :], mod, w_in[i].astype(BF16), gain, rope_cos, rope_sin)
        o = _attn_a(cfg, tl, p)
        o = _attn_b(cfg, tl, p, o, _na_bias(rpb[i], row_tables, col_table), starts, classes, win)
        if not last:
            o = _attn_ctx(cfg, p, o)
        live_tiles = n_lat_tiles if last else n_tiles
        x_all = _out_proj(cfg, tl, o, w_out[i].astype(BF16), x_all, mod, live_tiles)

        g_ffn = norm_ffn[i][None, :]
        if i % 2 == 0:
            x_all = _ffn(cfg, tl, x_all, g_ffn, mod, w1_dense[j][None].astype(BF16),
                         w3_dense[j][None].astype(BF16), w2_dense[j][None].astype(BF16), live_tiles)
        else:
            wr = jnp.zeros((d, LANE), F32).at[:, :cfg.n_experts].set(w_router[j])
            h, route = _router(cfg, tl, x_all, g_ffn, mod, wr, live_tiles)
            tile_expert, n_used, src_tok, pos_a, pos_b = _route_plan(cfg, tl.tm, route, live_tiles * tl.tm)
            y_sorted = _moe_ffn(cfg, tl, h, w1_moe[j].astype(BF16), w3_moe[j].astype(BF16),
                                w2_moe[j].astype(BF16), tile_expert, n_used, src_tok)
            x_all = _moe_combine(cfg, tl, y_sorted, x_all, mod, route, pos_a, pos_b, live_tiles)

    return x_all[:cfg.n_lat].reshape(cfg.batch, cfg.seq, d)


def kernel(x, c, ctx, c_ctx, ada_w, ada_b, norm_attn, norm_ffn, w_in, qn_a, kn_a, qn_b, kn_b, rpb, w_out,
           w1_dense, w3_dense, w2_dense, w_router, w1_moe, w3_moe, w2_moe):
    batch, seq, d_model = x.shape
    depth = w_in.shape[0]
    head_dim = qn_a.shape[-1]
    heads_b = rpb.shape[1]
    wb = heads_b * head_dim
    wa_q = w_out.shape[1] - wb
    wa_kv = (w_in.shape[2] - wa_q - 3 * wb) // 2
    cfg = Cfg(d_model=d_model, batch=batch, seq=seq, ctx_len=ctx.shape[1], grid_w=64, head_dim=head_dim,
              heads_a=wa_q // head_dim, kv_a=wa_kv // head_dim, heads_b=heads_b,
              na_kh=(rpb.shape[2] + 1) // 2, na_kw=(rpb.shape[3] + 1) // 2, d_ff=w1_dense.shape[-1],
              n_experts=w_router.shape[-1], depth=depth)
    return _forward(cfg, x, c, ctx, c_ctx, ada_w, ada_b, norm_attn, norm_ffn, w_in, qn_a, kn_a, qn_b, kn_b,
                    rpb, w_out, w1_dense, w3_dense, w2_dense, w_router, w1_moe, w3_moe, w2_moe)
```

```python
import functools
from typing import NamedTuple

import numpy as np
import jax
import jax.numpy as jnp
from jax import lax
from jax.experimental import pallas as pl
from jax.experimental.pallas import tpu as pltpu

F32 = jnp.float32
BF16 = jnp.bfloat16

EPS = 1e-6
ROPE_THETA = 10000.0
N_MOD = 6
LANE = 128
MASK_VALUE = -1e30
LOG2E = 1.4426950408889634
V7X_VMEM_LIMIT = 56 * 1024 * 1024


class Cfg(NamedTuple):
    d_model: int
    batch: int
    seq: int
    ctx_len: int
    grid_w: int
    head_dim: int
    heads_a: int
    kv_a: int
    heads_b: int
    na_kh: int
    na_kw: int
    d_ff: int
    n_experts: int
    depth: int

    @property
    def wa_q(self):
        return self.heads_a * self.head_dim

    @property
    def wa_kv(self):
        return self.kv_a * self.head_dim

    @property
    def wb(self):
        return self.heads_b * self.head_dim

    @property
    def mix_w(self):
        return self.wa_q + self.wb

    @property
    def in_cols(self):
        return self.mix_w + 2 * self.wa_kv + 2 * self.wb

    @property
    def rows(self):
        return self.seq // self.grid_w

    @property
    def n_lat(self):
        return self.batch * self.seq

    @property
    def n_tok(self):
        return self.batch * (self.seq + self.ctx_len)

    @property
    def off_ka(self):
        return self.mix_w

    @property
    def off_va(self):
        return self.mix_w + self.wa_kv

    @property
    def off_kb(self):
        return self.mix_w + 2 * self.wa_kv

    @property
    def off_vb(self):
        return self.mix_w + 2 * self.wa_kv + self.wb


class Tiles(NamedTuple):
    tm: int
    tm_proj: int
    hps: int
    tn_out: int
    tf: int
    tq: int
    rb: int
    tn_ada: int


def _pick_tiles(cfg):
    n_ctx = cfg.batch * cfg.ctx_len
    tm = min(512, cfg.seq, n_ctx)
    tm_proj = min(1024, cfg.seq, n_ctx)
    for t in (tm, tm_proj):
        assert cfg.seq % t == 0 and n_ctx % t == 0
    n_heads = cfg.in_cols // LANE
    hps = next(h for h in (12, 10, 8, 6, 4, 2) if n_heads % h == 0)
    tn_out = min(1024, cfg.d_model)
    tf = min(512, cfg.d_ff)
    assert cfg.d_model % tn_out == 0 and cfg.d_ff % tf == 0
    tq = min(256, cfg.seq)
    assert cfg.seq % tq == 0
    rb = 4
    assert cfg.rows % (2 * rb) == 0 and cfg.rows >= rb + cfg.na_kh
    tn_ada = next(t for t in (1024, 512, 256, LANE) if (N_MOD * cfg.d_model) % t == 0)
    return Tiles(tm, tm_proj, hps, tn_out, tf, tq, rb, tn_ada)


def _params(*sem):
    return pltpu.CompilerParams(dimension_semantics=sem, vmem_limit_bytes=V7X_VMEM_LIMIT)


def _dot(a, b):
    return jnp.dot(a, b, preferred_element_type=F32)


def _dot_nt(a, b):
    return lax.dot_general(a, b, (((1,), (1,)), ((), ())), preferred_element_type=F32)


def _rms(x):
    return x * lax.rsqrt(jnp.mean(x * x, axis=-1, keepdims=True) + EPS)


def _ada_kernel(cond_ref, w_ref, b_ref, o_ref):
    c = cond_ref[...]
    c = (c * jax.nn.sigmoid(c)).astype(BF16)
    o_ref[0] = _dot(c, w_ref[0].astype(BF16)) + b_ref[0]


def _ada_mod(cfg, tl, cond_pad, ada_w, ada_b):
    depth, d, n6 = ada_w.shape
    mp = cond_pad.shape[0]
    return pl.pallas_call(
        _ada_kernel,
        grid=(depth, n6 // tl.tn_ada),
        in_specs=[
            pl.BlockSpec((mp, d), lambda l, j: (0, 0)),
            pl.BlockSpec((1, d, tl.tn_ada), lambda l, j: (l, 0, j)),
            pl.BlockSpec((1, 1, tl.tn_ada), lambda l, j: (l, 0, j)),
        ],
        out_specs=pl.BlockSpec((1, mp, tl.tn_ada), lambda l, j: (l, 0, j)),
        out_shape=jax.ShapeDtypeStruct((depth, mp, n6), F32),
        compiler_params=_params("parallel", "parallel"),
        name="ada_mod",
    )(cond_pad, ada_w, ada_b.reshape(depth, 1, n6))


def _mod_index(cfg, tm):
    n_lat_tiles = cfg.n_lat // tm
    per_sample = cfg.seq // tm

    def f(i):
        return jnp.where(i < n_lat_tiles, i // per_sample, cfg.batch)
    return f


def _mod_spec(cfg, tm, k, ncols, grid_rank, col_axis=None):
    mi = _mod_index(cfg, tm)
    if col_axis is None:
        return pl.BlockSpec((1, 1, 1, ncols), lambda i, *rest: (mi(i), k, 0, 0))
    assert grid_rank == 2 and col_axis == 1
    return pl.BlockSpec((1, 1, 1, ncols), lambda i, j: (mi(i), k, 0, j))


def _modulated(x, g_ref, shift_ref, scale_ref):
    y = _rms(x) * g_ref[...]
    return y * (1.0 + scale_ref[0, 0]) + shift_ref[0, 0]


def _swap_halves(y):
    lane = lax.broadcasted_iota(jnp.int32, y.shape, 1)
    quarter = LANE // 4
    fwd = pltpu.roll(y, LANE - quarter, 1)
    bwd = pltpu.roll(y, quarter, 1)
    return jnp.where((lane % (2 * quarter)) < quarter, fwd, bwd)


HEAD_ROPE, HEAD_NORM, HEAD_PLAIN = "rope", "norm", "plain"


def _proj_kernel(x_ref, g_ref, shift_ref, scale_ref, w_ref, gain_ref, cos_ref, sin_ref, o_ref, h_ref,
                 *, step_kinds):
    j = pl.program_id(1)

    @pl.when(j == 0)
    def _():
        h_ref[...] = _modulated(x_ref[...], g_ref, shift_ref, scale_ref).astype(BF16)

    def run(kinds):
        for c in range(0, len(kinds), 2):
            acc = _dot(h_ref[...], w_ref[:, c * LANE:(c + 2) * LANE])
            for k in (c, c + 1):
                y = acc[:, (k - c) * LANE:(k - c + 1) * LANE]
                if kinds[k] != HEAD_PLAIN:
                    y = _rms(y) * gain_ref[:, k * LANE:(k + 1) * LANE]
                if kinds[k] == HEAD_ROPE:
                    y = y * cos_ref[...] + _swap_halves(y) * sin_ref[...]
                o_ref[:, k * LANE:(k + 1) * LANE] = y.astype(o_ref.dtype)

    for kinds in sorted(set(step_kinds)):
        steps = [s for s, kk in enumerate(step_kinds) if kk == kinds]
        cond = functools.reduce(jnp.logical_or, [j == s for s in steps])
        pl.when(cond)(functools.partial(run, kinds))


def _in_proj(cfg, tl, x_all, g, mod, w_in, gain, rope_cos, rope_sin):
    tm, tn = tl.tm_proj, tl.hps * LANE
    d = cfg.d_model
    n_tiles = cfg.n_tok // tm
    n_lat_tiles = cfg.n_lat // tm
    per_sample = cfg.seq // tm
    head_kinds = ([HEAD_ROPE] * cfg.heads_a + [HEAD_NORM] * cfg.heads_b + [HEAD_ROPE] * cfg.kv_a
                  + [HEAD_PLAIN] * cfg.kv_a + [HEAD_NORM] * cfg.heads_b + [HEAD_PLAIN] * cfg.heads_b)
    step_kinds = tuple(tuple(head_kinds[s:s + tl.hps]) for s in range(0, len(head_kinds), tl.hps))
    kern = functools.partial(_proj_kernel, step_kinds=step_kinds)

    def rope_idx(i, j):
        return (jnp.where(i < n_lat_tiles, i % per_sample, per_sample), 0)

    return pl.pallas_call(
        kern,
        grid=(n_tiles, cfg.in_cols // tn),
        in_specs=[
            pl.BlockSpec((tm, d), lambda i, j: (i, 0)),
            pl.BlockSpec((1, d), lambda i, j: (0, 0)),
            _mod_spec(cfg, tm, 0, d, 2),
            _mod_spec(cfg, tm, 1, d, 2),
            pl.BlockSpec((d, tn), lambda i, j: (0, j)),
            pl.BlockSpec((1, tn), lambda i, j: (0, j)),
            pl.BlockSpec((tm, LANE), rope_idx),
            pl.BlockSpec((tm, LANE), rope_idx),
        ],
        out_specs=pl.BlockSpec((tm, tn), lambda i, j: (i, j)),
        out_shape=jax.ShapeDtypeStruct((cfg.n_tok, cfg.in_cols), BF16),
        scratch_shapes=[pltpu.VMEM((tm, d), BF16)],
        compiler_params=_params("parallel", "arbitrary"),
        name="in_proj",
    )(x_all, g, mod, mod, w_in, gain, rope_cos, rope_sin)


def _softmax_pv(s_parts, v_parts):
    m = None
    for s in s_parts:
        mx = jnp.max(s, axis=-1, keepdims=True)
        m = mx if m is None else jnp.maximum(m, mx)
    l = None
    o = None
    for s, v in zip(s_parts, v_parts):
        p = jnp.exp2(s - m)
        ps = jnp.sum(p, axis=-1, keepdims=True)
        pv = _dot(p.astype(BF16), v)
        l = ps if l is None else l + ps
        o = pv if o is None else o + pv
    return o / l


def _softmax_pv_ext(s_parts, vext_parts):
    m = None
    for s in s_parts:
        mx = jnp.max(s, axis=-1, keepdims=True)
        m = mx if m is None else jnp.maximum(m, mx)
    oe = None
    for s, v in zip(s_parts, vext_parts):
        pv = _dot(jnp.exp2(s - m).astype(BF16), v)
        oe = pv if oe is None else oe + pv
    return oe[:, :LANE] / oe[:, LANE:]


def _softmax_pv_given_max(s, m, v_ext):
    oe = _dot(jnp.exp2(s - m).astype(BF16), v_ext)
    return oe[:, :LANE] / oe[:, LANE:]


def _fill_kv_ext(k_all, v_ext, kl_ref, vl_ref, kc_ref, vc_ref):
    n_lat = kl_ref.shape[0]
    k_all[:n_lat, :] = kl_ref[...]
    k_all[n_lat:, :] = kc_ref[...]
    v_ext[:n_lat, :LANE] = vl_ref[...]
    v_ext[n_lat:, :LANE] = vc_ref[...]
    v_ext[:, LANE:] = jnp.ones((v_ext.shape[0], LANE), v_ext.dtype)


def _attn_a_kernel(q_ref, kl_ref, vl_ref, kc_ref, vc_ref, o_ref, k_all, v_ext, s_buf, m_buf, *, group):
    tq = q_ref.shape[0]
    phase = pl.program_id(3)

    @pl.when(jnp.logical_and(pl.program_id(2) == 0, phase == 0))
    def _():
        _fill_kv_ext(k_all, v_ext, kl_ref, vl_ref, kc_ref, vc_ref)

    @pl.when(phase == 0)
    def _():
        q = jnp.concatenate([q_ref[:, h * LANE:(h + 1) * LANE] for h in range(group)], axis=0)
        s = _dot_nt(q, k_all[...])
        s_buf[...] = s
        m_buf[...] = jnp.max(s, axis=-1, keepdims=True)

    @pl.when(phase == 1)
    def _():
        o = _softmax_pv_given_max(s_buf[...], m_buf[...], v_ext[...])
        for h in range(group):
            o_ref[:, h * LANE:(h + 1) * LANE] = o[h * tq:(h + 1) * tq].astype(o_ref.dtype)


def _attn_a(cfg, tl, p):
    assert cfg.head_dim == LANE
    group = cfg.heads_a // cfg.kv_a
    gw = group * LANE
    tq = tl.tq
    nq = cfg.seq // tq
    ctx_blk0 = cfg.n_lat // cfg.ctx_len
    ka0, va0 = cfg.off_ka // LANE, cfg.off_va // LANE
    n_keys = cfg.seq + cfg.ctx_len
    return pl.pallas_call(
        functools.partial(_attn_a_kernel, group=group),
        grid=(cfg.batch, cfg.kv_a, nq, 2),
        scratch_shapes=[pltpu.VMEM((n_keys, LANE), BF16), pltpu.VMEM((n_keys, 2 * LANE), BF16),
                        pltpu.VMEM((group * tq, n_keys), F32), pltpu.VMEM((group * tq, 1), F32)],
        in_specs=[
            pl.BlockSpec((tq, gw), lambda b, g, i, ph: (b * nq + i, g)),
            pl.BlockSpec((cfg.seq, LANE), lambda b, g, i, ph: (b, ka0 + g)),
            pl.BlockSpec((cfg.seq, LANE), lambda b, g, i, ph: (b, va0 + g)),
            pl.BlockSpec((cfg.ctx_len, LANE), lambda b, g, i, ph: (ctx_blk0 + b, ka0 + g)),
            pl.BlockSpec((cfg.ctx_len, LANE), lambda b, g, i, ph: (ctx_blk0 + b, va0 + g)),
        ],
        out_specs=pl.BlockSpec((tq, gw), lambda b, g, i, ph: (b * nq + i, g)),
        out_shape=jax.ShapeDtypeStruct((cfg.n_tok, cfg.mix_w), BF16),
        compiler_params=_params("arbitrary", "arbitrary", "arbitrary", "arbitrary"),
        name="attn_global",
    )(p, p, p, p, p)


def _na_geometry(cfg, rb):
    rows, w = cfg.rows, cfg.grid_w
    kh = min(cfg.na_kh, rows)
    kw = cfg.na_kw
    win = min(rows, -(-(rb - 1 + kh) // rb) * rb)
    nblk = rows // rb
    r0 = np.clip(np.arange(rows) - kh // 2, 0, rows - kh)
    c0 = np.clip(np.arange(w) - kw // 2, 0, w - kw)
    starts = np.clip(rb * np.arange(nblk) - kh // 2, 0, rows - win)
    col = np.arange(w)
    vc = (col[None, :] >= c0[:, None]) & (col[None, :] < c0[:, None] + kw)
    dc = np.where(vc, col[None, :] - col[:, None] + (cfg.na_kw - 1), 0)

    def row_table(i):
        r = rb * i + np.arange(rb)
        key_r = starts[i] + np.arange(win)
        vr = (key_r[None, :] >= r0[r][:, None]) & (key_r[None, :] < r0[r][:, None] + kh)
        assert (vr.sum(axis=1) == kh).all()
        return vr, np.where(vr, key_r[None, :] - r[:, None] + (cfg.na_kh - 1), 0)

    row_tables, classes = [], []
    for i in range(nblk):
        t = row_table(i)
        for c, u in enumerate(row_tables):
            if all(np.array_equal(a, b) for a, b in zip(t, u)):
                classes.append(c)
                break
        else:
            classes.append(len(row_tables))
            row_tables.append(t)
    return win, starts, np.asarray(classes), row_tables, (vc, dc)


def _na_bias(rpb_l, row_tables, col_table):
    h, n_dr, n_dc = rpb_l.shape
    vc, dc = col_table
    w = vc.shape[0]
    onehot = (dc[None] == np.arange(n_dc)[:, None, None]).astype(np.float32)
    t = jnp.einsum("hdk,kqc->hdqc", rpb_l, onehot, precision=lax.Precision.HIGHEST)
    out = []
    for vr, dr in row_tables:
        rb, win = vr.shape
        b = jnp.take(t, dr.reshape(-1), axis=1).reshape(h, rb, win, w, w)
        valid = vr[:, :, None, None] & vc[None, None]
        b = jnp.where(valid[None], b * LOG2E, MASK_VALUE).transpose(0, 1, 3, 2, 4)
        out.append(b.reshape(h, rb * w, win * w))
    return jnp.stack(out, axis=1).astype(F32)


def _attn_b_kernel(start_ref, cls_ref, o_in_ref, q_ref, kl_ref, vl_ref, kc_ref, vc_ref, bias_ref, o_ref,
                   k_all, v_ext, s_buf, *, nblk, qb, kwin, w):
    del o_in_ref
    _fill_kv_ext(k_all, v_ext, kl_ref, vl_ref, kc_ref, vc_ref)
    n_lat = kl_ref.shape[0]

    def rows(i):
        return pl.ds(pl.multiple_of(i * qb, qb), qb)

    def window(i):
        return pl.ds(pl.multiple_of(start_ref[i] * w, w), kwin)

    def scores(i, carry):
        q = q_ref[rows(i), :]
        s_buf[i, :, :kwin] = _dot_nt(q, k_all[window(i), :]) + bias_ref[0, cls_ref[i]]
        s_buf[i, :, kwin:] = _dot_nt(q, k_all[n_lat:, :])
        return carry

    def outputs(i, carry):
        s = s_buf[i]
        p = jnp.exp2(s - jnp.max(s, axis=-1, keepdims=True)).astype(BF16)
        oe = _dot(p[:, :kwin], v_ext[window(i), :]) + _dot(p[:, kwin:], v_ext[n_lat:, :])
        o_ref[rows(i), :] = (oe[:, :LANE] / oe[:, LANE:]).astype(o_ref.dtype)
        return carry

    lax.fori_loop(0, nblk, scores, 0, unroll=4)
    lax.fori_loop(0, nblk, outputs, 0, unroll=2)


def _attn_b(cfg, tl, p, o, bias, starts, classes, win):
    w = cfg.grid_w
    qb, kwin = tl.rb * w, win * w
    nblk = cfg.rows // tl.rb
    ncls = bias.shape[1]
    ctx_blk0 = cfg.n_lat // cfg.ctx_len
    q0, k0, v0 = cfg.wa_q // LANE, cfg.off_kb // LANE, cfg.off_vb // LANE
    grid_spec = pltpu.PrefetchScalarGridSpec(
        num_scalar_prefetch=2,
        grid=(cfg.heads_b, cfg.batch),
        in_specs=[
            pl.BlockSpec(memory_space=pl.ANY),
            pl.BlockSpec((cfg.seq, LANE), lambda h, b, s, c: (b, q0 + h)),
            pl.BlockSpec((cfg.seq, LANE), lambda h, b, s, c: (b, k0 + h)),
            pl.BlockSpec((cfg.seq, LANE), lambda h, b, s, c: (b, v0 + h)),
            pl.BlockSpec((cfg.ctx_len, LANE), lambda h, b, s, c: (ctx_blk0 + b, k0 + h)),
            pl.BlockSpec((cfg.ctx_len, LANE), lambda h, b, s, c: (ctx_blk0 + b, v0 + h)),
            pl.BlockSpec((1, ncls, qb, kwin), lambda h, b, s, c: (h, 0, 0, 0)),
        ],
        out_specs=pl.BlockSpec((cfg.seq, LANE), lambda h, b, s, c: (b, q0 + h)),
        scratch_shapes=[pltpu.VMEM((cfg.seq + cfg.ctx_len, LANE), BF16),
                        pltpu.VMEM((cfg.seq + cfg.ctx_len, 2 * LANE), BF16),
                        pltpu.VMEM((nblk, qb, kwin + cfg.ctx_len), F32)],
    )
    return pl.pallas_call(
        functools.partial(_attn_b_kernel, nblk=nblk, qb=qb, kwin=kwin, w=w),
        grid_spec=grid_spec,
        out_shape=jax.ShapeDtypeStruct(o.shape, o.dtype),
        input_output_aliases={2: 0},
        compiler_params=_params("parallel", "arbitrary"),
        name="attn_neighbourhood",
    )(jnp.asarray(starts, jnp.int32), jnp.asarray(classes, jnp.int32), o, p, p, p, p, p, bias)


def _attn_ctx_kernel(o_in_ref, p_ref, o_ref, *, cfg):
    del o_in_ref
    group = cfg.heads_a // cfg.kv_a

    def cols(off, h):
        return p_ref[:, off + h * LANE: off + (h + 1) * LANE]

    for h in range(cfg.heads_a):
        g = h // group
        o = _softmax_pv([_dot_nt(cols(0, h), cols(cfg.off_ka, g))], [cols(cfg.off_va, g)])
        o_ref[:, h * LANE:(h + 1) * LANE] = o.astype(o_ref.dtype)
    for h in range(cfg.heads_b):
        o = _softmax_pv([_dot_nt(cols(cfg.wa_q, h), cols(cfg.off_kb, h))], [cols(cfg.off_vb, h)])
        o_ref[:, cfg.wa_q + h * LANE: cfg.wa_q + (h + 1) * LANE] = o.astype(o_ref.dtype)


def _attn_ctx(cfg, p, o):
    ctx_blk0 = cfg.n_lat // cfg.ctx_len
    return pl.pallas_call(
        functools.partial(_attn_ctx_kernel, cfg=cfg),
        grid=(cfg.batch,),
        in_specs=[
            pl.BlockSpec(memory_space=pl.ANY),
            pl.BlockSpec((cfg.ctx_len, cfg.in_cols), lambda b: (ctx_blk0 + b, 0)),
        ],
        out_specs=pl.BlockSpec((cfg.ctx_len, cfg.mix_w), lambda b: (ctx_blk0 + b, 0)),
        out_shape=jax.ShapeDtypeStruct(o.shape, o.dtype),
        input_output_aliases={0: 0},
        compiler_params=_params("parallel"),
        name="attn_context",
    )(o, p)


def _out_proj_kernel(o_ref, w_ref, x_ref, gate_ref, y_ref):
    half = y_ref.shape[1] // 2
    gate = gate_ref[0, 0]
    for c in (slice(0, half), slice(half, 2 * half)):
        y_ref[:, c] = x_ref[:, c] + gate[:, c] * _dot(o_ref[...], w_ref[:, c])


def _out_proj(cfg, tl, o, w_out, x_all, mod, n_rows):
    tm, tn = tl.tm_proj, tl.tn_out
    n_tiles = n_rows // tm
    return pl.pallas_call(
        _out_proj_kernel,
        grid=(n_tiles, cfg.d_model // tn),
        in_specs=[
            pl.BlockSpec((tm, cfg.mix_w), lambda i, j: (i, 0)),
            pl.BlockSpec((cfg.mix_w, tn), lambda i, j: (0, j)),
            pl.BlockSpec((tm, tn), lambda i, j: (i, j)),
            _mod_spec(cfg, tm, 2, tn, 2, col_axis=1),
        ],
        out_specs=pl.BlockSpec((tm, tn), lambda i, j: (i, j)),
        out_shape=jax.ShapeDtypeStruct(x_all.shape, F32),
        compiler_params=_params("parallel", "arbitrary"),
        name="out_proj",
    )(o, w_out, x_all, mod)


def _router_kernel(x_ref, g_ref, shift_ref, scale_ref, wr_ref, h_ref, route_ref, *, n_experts):
    h = _modulated(x_ref[...], g_ref, shift_ref, scale_ref)
    h_ref[...] = h
    logits = jnp.dot(h, wr_ref[...], preferred_element_type=F32, precision=lax.Precision.HIGHEST)
    lane = lax.broadcasted_iota(jnp.int32, logits.shape, 1).astype(F32)
    neg = jnp.float32(-jnp.inf)
    logits = jnp.where(lane < n_experts, logits, neg)
    v1 = jnp.max(logits, axis=-1, keepdims=True)
    i1 = jnp.min(jnp.where(logits == v1, lane, float(LANE)), axis=-1, keepdims=True)
    rest = jnp.where(lane == i1, neg, logits)
    v2 = jnp.max(rest, axis=-1, keepdims=True)
    i2 = jnp.min(jnp.where(rest == v2, lane, float(LANE)), axis=-1, keepdims=True)
    w1 = 1.0 / (1.0 + jnp.exp(v2 - v1))
    route = jnp.where(lane == 0.0, w1, jnp.where(lane == 1.0, 1.0 - w1, jnp.where(lane == 2.0, i1, i2)))
    route_ref[...] = jnp.where(lane < 4.0, route, 0.0)


def _router(cfg, tl, x_all, g, mod, w_router_pad, n_tiles):
    tm, d = tl.tm, cfg.d_model
    return pl.pallas_call(
        functools.partial(_router_kernel, n_experts=cfg.n_experts),
        grid=(n_tiles,),
        in_specs=[
            pl.BlockSpec((tm, d), lambda i: (i, 0)),
            pl.BlockSpec((1, d), lambda i: (0, 0)),
            _mod_spec(cfg, tm, 3, d, 1),
            _mod_spec(cfg, tm, 4, d, 1),
            pl.BlockSpec((d, LANE), lambda i: (0, 0)),
        ],
        out_specs=[pl.BlockSpec((tm, d), lambda i: (i, 0)), pl.BlockSpec((tm, LANE), lambda i: (i, 0))],
        out_shape=[jax.ShapeDtypeStruct((n_tiles * tm, d), F32), jax.ShapeDtypeStruct((n_tiles * tm, LANE), F32)],
        compiler_params=_params("parallel"),
        name="router",
    )(x_all, g, mod, mod, w_router_pad)


def _swiglu_step(h, w1_ref, w3_ref, w2_ref):
    a = _dot(h, w1_ref[0])
    b = _dot(h, w3_ref[0])
    return _dot((a * jax.nn.sigmoid(a) * b).astype(BF16), w2_ref[0])


def _ffn_kernel(x_ref, g_ref, shift_ref, scale_ref, gate_ref, w1_ref, w3_ref, w2_ref, y_ref, h_ref):
    f = pl.program_id(1)

    @pl.when(f == 0)
    def _():
        h_ref[...] = _modulated(x_ref[...], g_ref, shift_ref, scale_ref).astype(BF16)
        y_ref[...] = jnp.zeros_like(y_ref)

    y_ref[...] += _swiglu_step(h_ref[...], w1_ref, w3_ref, w2_ref)

    @pl.when(f == pl.num_programs(1) - 1)
    def _():
        y_ref[...] = x_ref[...] + gate_ref[0, 0] * y_ref[...]


def _ffn(cfg, tl, x_all, g, mod, w1, w3, w2, n_tiles):
    tm, tf, d = tl.tm, tl.tf, cfg.d_model
    return pl.pallas_call(
        _ffn_kernel,
        grid=(n_tiles, cfg.d_ff // tf),
        in_specs=[
            pl.BlockSpec((tm, d), lambda i, f: (i, 0)),
            pl.BlockSpec((1, d), lambda i, f: (0, 0)),
            _mod_spec(cfg, tm, 3, d, 2),
            _mod_spec(cfg, tm, 4, d, 2),
            _mod_spec(cfg, tm, 5, d, 2),
            pl.BlockSpec((1, d, tf), lambda i, f: (0, 0, f)),
            pl.BlockSpec((1, d, tf), lambda i, f: (0, 0, f)),
            pl.BlockSpec((1, tf, d), lambda i, f: (0, f, 0)),
        ],
        out_specs=pl.BlockSpec((tm, d), lambda i, f: (i, 0)),
        out_shape=jax.ShapeDtypeStruct(x_all.shape, F32),
        scratch_shapes=[pltpu.VMEM((tm, d), BF16)],
        compiler_params=_params("parallel", "arbitrary"),
        name="ffn_dense",
    )(x_all, g, mod, mod, mod, w1, w3, w2)


def _route_plan(cfg, tm, route, n_rows):
    n_e = cfg.n_experts
    n_assign = 2 * n_rows
    n_tiles_max = n_assign // tm + n_e
    expert = route[:, 2:4].astype(jnp.int32).reshape(n_assign)
    onehot = (expert[:, None] == jnp.arange(n_e)[None, :]).astype(jnp.int32)
    csum = jnp.cumsum(onehot, axis=0)
    rank = jnp.sum(csum * onehot, axis=1) - 1
    counts = csum[-1]
    tiles_per_e = (counts + tm - 1) // tm
    tile_end = jnp.cumsum(tiles_per_e)
    tile_start = tile_end - tiles_per_e
    n_used = tile_end[-1]
    pos = jnp.sum(onehot * tile_start[None, :], axis=1) * tm + rank
    tile_ids = jnp.arange(n_tiles_max)
    tile_expert = jnp.minimum(jnp.sum(tile_ids[:, None] >= tile_end[None, :], axis=1), n_e - 1)
    tile_expert = jnp.where(tile_ids < n_used, tile_expert, tile_expert[jnp.maximum(n_used - 1, 0)])
    order = jnp.argsort(expert, stable=True)
    unpadded_start = jnp.cumsum(counts) - counts
    k = (tile_ids - tile_start[tile_expert])[:, None] * tm + jnp.arange(tm)[None, :]
    valid = jnp.logical_and(k < counts[tile_expert][:, None], (tile_ids < n_used)[:, None])
    src = order[jnp.clip(unpadded_start[tile_expert][:, None] + k, 0, n_assign - 1)] // 2
    src_tok = jnp.where(valid, src, 0).astype(jnp.int32).reshape(n_tiles_max, 1, tm)
    pos = pos.astype(jnp.int32).reshape(n_rows // tm, 1, tm, 2)
    return (tile_expert.astype(jnp.int32), n_used.astype(jnp.int32).reshape(1), src_tok,
            pos[..., 0], pos[..., 1])


def _start_row_gather(idx_ref, src_hbm, dst_ref, sem, n_rows):
    def body(r, carry):
        pltpu.make_async_copy(src_hbm.at[pl.ds(idx_ref[0, 0, r], 1), :], dst_ref.at[pl.ds(r, 1), :], sem).start()
        return carry
    lax.fori_loop(0, n_rows, body, 0, unroll=8)


def _wait_row_gather(src_hbm, dst_ref, sem, n_rows):
    def body(r, carry):
        pltpu.make_async_copy(src_hbm.at[pl.ds(0, 1), :], dst_ref.at[pl.ds(r, 1), :], sem).wait()
        return carry
    lax.fori_loop(0, n_rows, body, 0, unroll=8)


def _moe_ffn_kernel(te_ref, nu_ref, src0_ref, src1_ref, h_hbm, w1_ref, w3_ref, w2_ref, y_ref,
                    hbuf, h_ref, sem, *, tm):
    del te_ref
    t, f = pl.program_id(0), pl.program_id(1)
    n_used = nu_ref[0]
    slot = lax.rem(t, 2)
    live = t < n_used
    at_tile_start = f == 0

    @pl.when(jnp.logical_and(at_tile_start, t == 0))
    def _():
        _start_row_gather(src0_ref, h_hbm, hbuf.at[0], sem.at[0], tm)

    @pl.when(jnp.logical_and(at_tile_start, t + 1 < n_used))
    def _():
        _start_row_gather(src1_ref, h_hbm, hbuf.at[1 - slot], sem.at[1 - slot], tm)

    @pl.when(jnp.logical_and(at_tile_start, live))
    def _():
        _wait_row_gather(h_hbm, hbuf.at[slot], sem.at[slot], tm)
        h_ref[...] = hbuf[slot].astype(BF16)
        y_ref[...] = jnp.zeros_like(y_ref)

    @pl.when(live)
    def _():
        y_ref[...] += _swiglu_step(h_ref[...], w1_ref, w3_ref, w2_ref)


def _moe_ffn(cfg, tl, h, w1, w3, w2, tile_expert, n_used, src_tok):
    tm, tf, d = tl.tm, tl.tf, cfg.d_model
    n_tiles_max = src_tok.shape[0]
    nf = cfg.d_ff // tf

    def live_tile(t, nu):
        return jnp.minimum(t, nu[0] - 1)

    def f_idx(t, f, nu):
        return jnp.where(t < nu[0], f, nf - 1)

    grid_spec = pltpu.PrefetchScalarGridSpec(
        num_scalar_prefetch=2,
        grid=(n_tiles_max, nf),
        in_specs=[
            pl.BlockSpec((1, 1, tm), lambda t, f, te, nu: (0, 0, 0), memory_space=pltpu.SMEM),
            pl.BlockSpec((1, 1, tm), lambda t, f, te, nu: (live_tile(t + 1, nu), 0, 0), memory_space=pltpu.SMEM),
            pl.BlockSpec(memory_space=pl.ANY),
            pl.BlockSpec((1, d, tf), lambda t, f, te, nu: (te[t], 0, f_idx(t, f, nu))),
            pl.BlockSpec((1, d, tf), lambda t, f, te, nu: (te[t], 0, f_idx(t, f, nu))),
            pl.BlockSpec((1, tf, d), lambda t, f, te, nu: (te[t], f_idx(t, f, nu), 0)),
        ],
        out_specs=pl.BlockSpec((tm, d), lambda t, f, te, nu: (live_tile(t, nu), 0)),
        scratch_shapes=[pltpu.VMEM((2, tm, d), F32), pltpu.VMEM((tm, d), BF16), pltpu.SemaphoreType.DMA((2,))],
    )
    return pl.pallas_call(
        functools.partial(_moe_ffn_kernel, tm=tm),
        grid_spec=grid_spec,
        out_shape=jax.ShapeDtypeStruct((n_tiles_max * tm, d), F32),
        compiler_params=_params("arbitrary", "arbitrary"),
        name="moe_ffn",
    )(tile_expert, n_used, src_tok, src_tok, h, w1, w3, w2)


def _moe_combine_kernel(pa0_ref, pb0_ref, pa1_ref, pb1_ref, y_hbm, x_ref, gate_ref, route_ref, o_ref,
                        ya, yb, sem, *, tm):
    i = pl.program_id(0)
    slot = lax.rem(i, 2)

    def start(pa_ref, pb_ref, s):
        _start_row_gather(pa_ref, y_hbm, ya.at[s], sem.at[s], tm)
        _start_row_gather(pb_ref, y_hbm, yb.at[s], sem.at[s], tm)

    @pl.when(i == 0)
    def _():
        start(pa0_ref, pb0_ref, 0)

    @pl.when(i + 1 < pl.num_programs(0))
    def _():
        start(pa1_ref, pb1_ref, 1 - slot)

    _wait_row_gather(y_hbm, ya.at[slot], sem.at[slot], tm)
    _wait_row_gather(y_hbm, yb.at[slot], sem.at[slot], tm)
    route = route_ref[...]
    y = route[:, 0:1] * ya[slot] + route[:, 1:2] * yb[slot]
    o_ref[...] = x_ref[...] + gate_ref[0, 0] * y


def _moe_combine(cfg, tl, y_sorted, x_all, mod, route, pos_a, pos_b, n_tiles):
    tm, d = tl.tm, cfg.d_model
    first = pl.BlockSpec((1, 1, tm), lambda i: (0, 0, 0), memory_space=pltpu.SMEM)
    nxt = pl.BlockSpec((1, 1, tm), lambda i: (jnp.minimum(i + 1, n_tiles - 1), 0, 0), memory_space=pltpu.SMEM)
    return pl.pallas_call(
        functools.partial(_moe_combine_kernel, tm=tm),
        grid=(n_tiles,),
        in_specs=[
            first, first, nxt, nxt,
            pl.BlockSpec(memory_space=pl.ANY),
            pl.BlockSpec((tm, d), lambda i: (i, 0)),
            _mod_spec(cfg, tm, 5, d, 1),
            pl.BlockSpec((tm, LANE), lambda i: (i, 0)),
        ],
        out_specs=pl.BlockSpec((tm, d), lambda i: (i, 0)),
        out_shape=jax.ShapeDtypeStruct(x_all.shape, F32),
        scratch_shapes=[pltpu.VMEM((2, tm, d), F32), pltpu.VMEM((2, tm, d), F32), pltpu.SemaphoreType.DMA((2,))],
        compiler_params=_params("arbitrary"),
        name="moe_combine",
    )(pos_a, pos_b, pos_a, pos_b, y_sorted, x_all, mod, route)


def _rope_tables(cfg, tm):
    half = cfg.head_dim // 2
    quarter = half // 2
    t = jnp.arange(cfg.seq)
    freqs = ROPE_THETA ** (-jnp.arange(quarter, dtype=F32) / quarter)
    ang_r = (t // cfg.grid_w).astype(F32)[:, None] * freqs[None, :]
    ang_c = (t % cfg.grid_w).astype(F32)[:, None] * freqs[None, :]
    cos = jnp.concatenate([jnp.cos(ang_r)] * 2 + [jnp.cos(ang_c)] * 2, axis=-1)
    sin = jnp.concatenate([-jnp.sin(ang_r), jnp.sin(ang_r), -jnp.sin(ang_c), jnp.sin(ang_c)], axis=-1)
    cos = jnp.concatenate([cos, jnp.ones((tm, cfg.head_dim), F32)], axis=0)
    sin = jnp.concatenate([sin, jnp.zeros((tm, cfg.head_dim), F32)], axis=0)
    return cos, sin


def _forward(cfg, x, c, ctx, c_ctx, ada_w, ada_b, norm_attn, norm_ffn, w_in, qn_a, kn_a, qn_b, kn_b, rpb,
             w_out, w1_dense, w3_dense, w2_dense, w_router, w1_moe, w3_moe, w2_moe):
    tl = _pick_tiles(cfg)
    d = cfg.d_model
    n_tiles = cfg.n_tok // tl.tm
    n_lat_tiles = cfg.n_lat // tl.tm

    x_all = jnp.concatenate([x.reshape(cfg.n_lat, d), ctx.reshape(cfg.batch * cfg.ctx_len, d)], axis=0)

    cond = jnp.concatenate([c, c_ctx[None, :]], axis=0)
    pad_rows = -(-cond.shape[0] // 16) * 16
    cond_pad = jnp.zeros((pad_rows, d), F32).at[:cond.shape[0]].set(cond)
    mod_all = _ada_mod(cfg, tl, cond_pad, ada_w, ada_b)

    rope_cos, rope_sin = _rope_tables(cfg, tl.tm_proj)
    win, starts, classes, row_tables, col_table = _na_geometry(cfg, tl.rb)
    q_scale = cfg.head_dim ** -0.5 * LOG2E
    ones_a = jnp.ones((cfg.wa_kv,), F32)
    ones_b = jnp.ones((cfg.wb,), F32)

    for i in range(cfg.depth):
        last = i == cfg.depth - 1
        j = i // 2
        mod = mod_all[i, :cfg.batch + 1].reshape(cfg.batch + 1, N_MOD, 1, d)
        gain = jnp.concatenate([
            jnp.tile(qn_a[i], cfg.heads_a) * q_scale, jnp.tile(qn_b[i], cfg.heads_b) * q_scale,
            jnp.tile(kn_a[i], cfg.kv_a), ones_a, jnp.tile(kn_b[i], cfg.heads_b), ones_b])[None, :]

        p = _in_proj(cfg, tl, x_all, norm_attn[i][None, :], mod, w_in[i].astype(BF16), gain, rope_cos, rope_sin)
        o = _attn_a(cfg, tl, p)
        o = _attn_b(cfg, tl, p, o, _na_bias(rpb[i], row_tables, col_table), starts, classes, win)
        if not last:
            o = _attn_ctx(cfg, p, o)
        live_tiles = n_lat_tiles if last else n_tiles
        x_all = _out_proj(cfg, tl, o, w_out[i].astype(BF16), x_all, mod, live_tiles * tl.tm)

        g_ffn = norm_ffn[i][None, :]
        if i % 2 == 0:
            x_all = _ffn(cfg, tl, x_all, g_ffn, mod, w1_dense[j][None].astype(BF16),
                         w3_dense[j][None].astype(BF16), w2_dense[j][None].astype(BF16), live_tiles)
        else:
            wr = jnp.zeros((d, LANE), F32).at[:, :cfg.n_experts].set(w_router[j])
            h, route = _router(cfg, tl, x_all, g_ffn, mod, wr, live_tiles)
            tile_expert, n_used, src_tok, pos_a, pos_b = _route_plan(cfg, tl.tm, route, live_tiles * tl.tm)
            y_sorted = _moe_ffn(cfg, tl, h, w1_moe[j].astype(BF16), w3_moe[j].astype(BF16),
                                w2_moe[j].astype(BF16), tile_expert, n_used, src_tok)
            x_all = _moe_combine(cfg, tl, y_sorted, x_all, mod, route, pos_a, pos_b, live_tiles)

    return x_all[:cfg.n_lat].reshape(cfg.batch, cfg.seq, d)


def kernel(x, c, ctx, c_ctx, ada_w, ada_b, norm_attn, norm_ffn, w_in, qn_a, kn_a, qn_b, kn_b, rpb, w_out,
           w1_dense, w3_dense, w2_dense, w_router, w1_moe, w3_moe, w2_moe):
    batch, seq, d_model = x.shape
    depth = w_in.shape[0]
    head_dim = qn_a.shape[-1]
    heads_b = rpb.shape[1]
    wb = heads_b * head_dim
    wa_q = w_out.shape[1] - wb
    wa_kv = (w_in.shape[2] - wa_q - 3 * wb) // 2
    cfg = Cfg(d_model=d_model, batch=batch, seq=seq, ctx_len=ctx.shape[1], grid_w=64, head_dim=head_dim,
              heads_a=wa_q // head_dim, kv_a=wa_kv // head_dim, heads_b=heads_b,
              na_kh=(rpb.shape[2] + 1) // 2, na_kw=(rpb.shape[3] + 1) // 2, d_ff=w1_dense.shape[-1],
              n_experts=w_router.shape[-1], depth=depth)
    return _forward(cfg, x, c, ctx, c_ctx, ada_w, ada_b, norm_attn, norm_ffn, w_in, qn_a, kn_a, qn_b, kn_b,
                    rpb, w_out, w1_dense, w3_dense, w2_dense, w_router, w1_moe, w3_moe, w2_moe)
```

```python
import functools
from typing import NamedTuple

import numpy as np
import jax
import jax.numpy as jnp
from jax import lax
from jax.experimental import pallas as pl
from jax.experimental.pallas import tpu as pltpu

F32 = jnp.float32
BF16 = jnp.bfloat16

EPS = 1e-6
ROPE_THETA = 10000.0
N_MOD = 6
LANE = 128
MASK_VALUE = -1e30
LOG2E = 1.4426950408889634
V7X_VMEM_LIMIT = 56 * 1024 * 1024


class Cfg(NamedTuple):
    d_model: int
    batch: int
    seq: int
    ctx_len: int
    grid_w: int
    head_dim: int
    heads_a: int
    kv_a: int
    heads_b: int
    na_kh: int
    na_kw: int
    d_ff: int
    n_experts: int
    depth: int

    @property
    def wa_q(self):
        return self.heads_a * self.head_dim

    @property
    def wa_kv(self):
        return self.kv_a * self.head_dim

    @property
    def wb(self):
        return self.heads_b * self.head_dim

    @property
    def mix_w(self):
        return self.wa_q + self.wb

    @property
    def in_cols(self):
        return self.mix_w + 2 * self.wa_kv + 2 * self.wb

    @property
    def rows(self):
        return self.seq // self.grid_w

    @property
    def n_lat(self):
        return self.batch * self.seq

    @property
    def n_tok(self):
        return self.batch * (self.seq + self.ctx_len)

    @property
    def off_ka(self):
        return self.mix_w

    @property
    def off_va(self):
        return self.mix_w + self.wa_kv

    @property
    def off_kb(self):
        return self.mix_w + 2 * self.wa_kv

    @property
    def off_vb(self):
        return self.mix_w + 2 * self.wa_kv + self.wb


class Tiles(NamedTuple):
    tm: int
    tm_proj: int
    hps: int
    tn_out: int
    tf: int
    tq: int
    rb: int
    tn_ada: int


def _pick_tiles(cfg):
    n_ctx = cfg.batch * cfg.ctx_len
    tm = min(512, cfg.seq, n_ctx)
    tm_proj = min(1024, cfg.seq, n_ctx)
    for t in (tm, tm_proj):
        assert cfg.seq % t == 0 and n_ctx % t == 0
    n_heads = cfg.in_cols // LANE
    hps = next(h for h in (12, 10, 8, 6, 4, 2) if n_heads % h == 0)
    tn_out = min(1024, cfg.d_model)
    tf = min(512, cfg.d_ff)
    assert cfg.d_model % tn_out == 0 and cfg.d_ff % tf == 0
    tq = min(256, cfg.seq)
    assert cfg.seq % tq == 0
    rb = 4
    assert cfg.rows % (2 * rb) == 0 and cfg.rows >= rb + cfg.na_kh
    tn_ada = next(t for t in (1024, 512, 256, LANE) if (N_MOD * cfg.d_model) % t == 0)
    return Tiles(tm, tm_proj, hps, tn_out, tf, tq, rb, tn_ada)


def _params(*sem):
    return pltpu.CompilerParams(dimension_semantics=sem, vmem_limit_bytes=V7X_VMEM_LIMIT)


def _dot(a, b):
    return jnp.dot(a, b, preferred_element_type=F32)


def _dot_nt(a, b):
    return lax.dot_general(a, b, (((1,), (1,)), ((), ())), preferred_element_type=F32)


def _rms(x):
    return x * lax.rsqrt(jnp.mean(x * x, axis=-1, keepdims=True) + EPS)


def _ada_kernel(cond_ref, w_ref, b_ref, o_ref):
    c = cond_ref[...]
    c = (c * jax.nn.sigmoid(c)).astype(BF16)
    o_ref[0] = _dot(c, w_ref[0].astype(BF16)) + b_ref[0]


def _ada_mod(cfg, tl, cond_pad, ada_w, ada_b):
    depth, d, n6 = ada_w.shape
    mp = cond_pad.shape[0]
    return pl.pallas_call(
        _ada_kernel,
        grid=(depth, n6 // tl.tn_ada),
        in_specs=[
            pl.BlockSpec((mp, d), lambda l, j: (0, 0)),
            pl.BlockSpec((1, d, tl.tn_ada), lambda l, j: (l, 0, j)),
            pl.BlockSpec((1, 1, tl.tn_ada), lambda l, j: (l, 0, j)),
        ],
        out_specs=pl.BlockSpec((1, mp, tl.tn_ada), lambda l, j: (l, 0, j)),
        out_shape=jax.ShapeDtypeStruct((depth, mp, n6), F32),
        compiler_params=_params("parallel", "parallel"),
        name="ada_mod",
    )(cond_pad, ada_w, ada_b.reshape(depth, 1, n6))


def _mod_index(cfg, tm):
    n_lat_tiles = cfg.n_lat // tm
    per_sample = cfg.seq // tm

    def f(i):
        return jnp.where(i < n_lat_tiles, i // per_sample, cfg.batch)
    return f


def _mod_spec(cfg, tm, k, ncols, grid_rank, col_axis=None):
    mi = _mod_index(cfg, tm)
    if col_axis is None:
        return pl.BlockSpec((1, 1, 1, ncols), lambda i, *rest: (mi(i), k, 0, 0))
    assert grid_rank == 2 and col_axis == 1
    return pl.BlockSpec((1, 1, 1, ncols), lambda i, j: (mi(i), k, 0, j))


def _modulated(x, g_ref, shift_ref, scale_ref):
    y = _rms(x) * g_ref[...]
    return y * (1.0 + scale_ref[0, 0]) + shift_ref[0, 0]


def _swap_halves(y):
    lane = lax.broadcasted_iota(jnp.int32, y.shape, 1)
    quarter = LANE // 4
    fwd = pltpu.roll(y, LANE - quarter, 1)
    bwd = pltpu.roll(y, quarter, 1)
    return jnp.where((lane % (2 * quarter)) < quarter, fwd, bwd)


HEAD_ROPE, HEAD_NORM, HEAD_PLAIN = "rope", "norm", "plain"


def _proj_kernel(x_ref, g_ref, shift_ref, scale_ref, w_ref, gain_ref, cos_ref, sin_ref, o_ref, h_ref,
                 *, step_kinds):
    j = pl.program_id(1)

    @pl.when(j == 0)
    def _():
        h_ref[...] = _modulated(x_ref[...], g_ref, shift_ref, scale_ref).astype(BF16)

    def run(kinds):
        for c in range(0, len(kinds), 2):
            acc = _dot(h_ref[...], w_ref[:, c * LANE:(c + 2) * LANE])
            for k in (c, c + 1):
                y = acc[:, (k - c) * LANE:(k - c + 1) * LANE]
                if kinds[k] != HEAD_PLAIN:
                    y = _rms(y) * gain_ref[:, k * LANE:(k + 1) * LANE]
                if kinds[k] == HEAD_ROPE:
                    y = y * cos_ref[...] + _swap_halves(y) * sin_ref[...]
                o_ref[:, k * LANE:(k + 1) * LANE] = y.astype(o_ref.dtype)

    for kinds in sorted(set(step_kinds)):
        steps = [s for s, kk in enumerate(step_kinds) if kk == kinds]
        cond = functools.reduce(jnp.logical_or, [j == s for s in steps])
        pl.when(cond)(functools.partial(run, kinds))


def _in_proj(cfg, tl, x_all, g, mod, w_in, layer, gain, rope_cos, rope_sin):
    tm, tn = tl.tm_proj, tl.hps * LANE
    d = cfg.d_model
    n_tiles = cfg.n_tok // tm
    n_lat_tiles = cfg.n_lat // tm
    per_sample = cfg.seq // tm
    head_kinds = ([HEAD_ROPE] * cfg.heads_a + [HEAD_NORM] * cfg.heads_b + [HEAD_ROPE] * cfg.kv_a
                  + [HEAD_PLAIN] * cfg.kv_a + [HEAD_NORM] * cfg.heads_b + [HEAD_PLAIN] * cfg.heads_b)
    step_kinds = tuple(tuple(head_kinds[s:s + tl.hps]) for s in range(0, len(head_kinds), tl.hps))
    kern = functools.partial(_proj_kernel, step_kinds=step_kinds)

    def rope_idx(i, j):
        return (jnp.where(i < n_lat_tiles, i % per_sample, per_sample), 0)

    return pl.pallas_call(
        kern,
        grid=(n_tiles, cfg.in_cols // tn),
        in_specs=[
            pl.BlockSpec((tm, d), lambda i, j: (i, 0)),
            pl.BlockSpec((1, d), lambda i, j: (0, 0)),
            _mod_spec(cfg, tm, 0, d, 2),
            _mod_spec(cfg, tm, 1, d, 2),
            pl.BlockSpec((None, d, tn), lambda i, j: (layer, 0, j)),
            pl.BlockSpec((1, tn), lambda i, j: (0, j)),
            pl.BlockSpec((tm, LANE), rope_idx),
            pl.BlockSpec((tm, LANE), rope_idx),
        ],
        out_specs=pl.BlockSpec((tm, tn), lambda i, j: (i, j)),
        out_shape=jax.ShapeDtypeStruct((cfg.n_tok, cfg.in_cols), BF16),
        scratch_shapes=[pltpu.VMEM((tm, d), BF16)],
        compiler_params=_params("parallel", "arbitrary"),
        name="in_proj",
    )(x_all, g, mod, mod, w_in, gain, rope_cos, rope_sin)


def _softmax_pv(s_parts, v_parts):
    m = None
    for s in s_parts:
        mx = jnp.max(s, axis=-1, keepdims=True)
        m = mx if m is None else jnp.maximum(m, mx)
    l = None
    o = None
    for s, v in zip(s_parts, v_parts):
        p = jnp.exp2(s - m)
        ps = jnp.sum(p, axis=-1, keepdims=True)
        pv = _dot(p.astype(BF16), v)
        l = ps if l is None else l + ps
        o = pv if o is None else o + pv
    return o / l


def _softmax_pv_ext(s_parts, vext_parts):
    m = None
    for s in s_parts:
        mx = jnp.max(s, axis=-1, keepdims=True)
        m = mx if m is None else jnp.maximum(m, mx)
    oe = None
    for s, v in zip(s_parts, vext_parts):
        pv = _dot(jnp.exp2(s - m).astype(BF16), v)
        oe = pv if oe is None else oe + pv
    return oe[:, :LANE] / oe[:, LANE:]


def _softmax_pv_given_max(s, m, v_ext):
    oe = _dot(jnp.exp2(s - m).astype(BF16), v_ext)
    return oe[:, :LANE] / oe[:, LANE:]


def _fill_kv_ext(k_all, v_ext, kl_ref, vl_ref, kc_ref, vc_ref):
    n_lat = kl_ref.shape[0]
    k_all[:n_lat, :] = kl_ref[...]
    k_all[n_lat:, :] = kc_ref[...]
    v_ext[:n_lat, :LANE] = vl_ref[...]
    v_ext[n_lat:, :LANE] = vc_ref[...]
    v_ext[:, LANE:] = jnp.ones((v_ext.shape[0], LANE), v_ext.dtype)


def _attn_a_kernel(q_ref, kl_ref, vl_ref, kc_ref, vc_ref, o_ref, k_all, v_ext, s_buf, m_buf, *, group):
    tq = q_ref.shape[0]
    phase = pl.program_id(3)

    @pl.when(jnp.logical_and(pl.program_id(2) == 0, phase == 0))
    def _():
        _fill_kv_ext(k_all, v_ext, kl_ref, vl_ref, kc_ref, vc_ref)

    @pl.when(phase == 0)
    def _():
        q = jnp.concatenate([q_ref[:, h * LANE:(h + 1) * LANE] for h in range(group)], axis=0)
        s = _dot_nt(q, k_all[...])
        s_buf[...] = s
        m_buf[...] = jnp.max(s, axis=-1, keepdims=True)

    @pl.when(phase == 1)
    def _():
        o = _softmax_pv_given_max(s_buf[...], m_buf[...], v_ext[...])
        for h in range(group):
            o_ref[:, h * LANE:(h + 1) * LANE] = o[h * tq:(h + 1) * tq].astype(o_ref.dtype)


def _attn_a(cfg, tl, p):
    assert cfg.head_dim == LANE
    group = cfg.heads_a // cfg.kv_a
    gw = group * LANE
    tq = tl.tq
    nq = cfg.seq // tq
    ctx_blk0 = cfg.n_lat // cfg.ctx_len
    ka0, va0 = cfg.off_ka // LANE, cfg.off_va // LANE
    n_keys = cfg.seq + cfg.ctx_len
    return pl.pallas_call(
        functools.partial(_attn_a_kernel, group=group),
        grid=(cfg.batch, cfg.kv_a, nq, 2),
        scratch_shapes=[pltpu.VMEM((n_keys, LANE), BF16), pltpu.VMEM((n_keys, 2 * LANE), BF16),
                        pltpu.VMEM((group * tq, n_keys), F32), pltpu.VMEM((group * tq, 1), F32)],
        in_specs=[
            pl.BlockSpec((tq, gw), lambda b, g, i, ph: (b * nq + i, g)),
            pl.BlockSpec((cfg.seq, LANE), lambda b, g, i, ph: (b, ka0 + g)),
            pl.BlockSpec((cfg.seq, LANE), lambda b, g, i, ph: (b, va0 + g)),
            pl.BlockSpec((cfg.ctx_len, LANE), lambda b, g, i, ph: (ctx_blk0 + b, ka0 + g)),
            pl.BlockSpec((cfg.ctx_len, LANE), lambda b, g, i, ph: (ctx_blk0 + b, va0 + g)),
        ],
        out_specs=pl.BlockSpec((tq, gw), lambda b, g, i, ph: (b * nq + i, g)),
        out_shape=jax.ShapeDtypeStruct((cfg.n_tok, cfg.mix_w), BF16),
        compiler_params=_params("arbitrary", "arbitrary", "arbitrary", "arbitrary"),
        name="attn_global",
    )(p, p, p, p, p)


def _na_geometry(cfg, rb):
    rows, w = cfg.rows, cfg.grid_w
    kh = min(cfg.na_kh, rows)
    kw = cfg.na_kw
    win = min(rows, -(-(rb - 1 + kh) // rb) * rb)
    nblk = rows // rb
    r0 = np.clip(np.arange(rows) - kh // 2, 0, rows - kh)
    c0 = np.clip(np.arange(w) - kw // 2, 0, w - kw)
    starts = np.clip(rb * np.arange(nblk) - kh // 2, 0, rows - win)
    col = np.arange(w)
    vc = (col[None, :] >= c0[:, None]) & (col[None, :] < c0[:, None] + kw)
    dc = np.where(vc, col[None, :] - col[:, None] + (cfg.na_kw - 1), 0)

    def row_table(i):
        r = rb * i + np.arange(rb)
        key_r = starts[i] + np.arange(win)
        vr = (key_r[None, :] >= r0[r][:, None]) & (key_r[None, :] < r0[r][:, None] + kh)
        assert (vr.sum(axis=1) == kh).all()
        return vr, np.where(vr, key_r[None, :] - r[:, None] + (cfg.na_kh - 1), 0)

    row_tables, classes = [], []
    for i in range(nblk):
        t = row_table(i)
        for c, u in enumerate(row_tables):
            if all(np.array_equal(a, b) for a, b in zip(t, u)):
                classes.append(c)
                break
        else:
            classes.append(len(row_tables))
            row_tables.append(t)
    return win, starts, np.asarray(classes), row_tables, (vc, dc)


def _na_bias(rpb_l, row_tables, col_table):
    h, n_dr, n_dc = rpb_l.shape
    vc, dc = col_table
    w = vc.shape[0]
    onehot = (dc[None] == np.arange(n_dc)[:, None, None]).astype(np.float32)
    t = jnp.einsum("hdk,kqc->hdqc", rpb_l, onehot, precision=lax.Precision.HIGHEST)
    out = []
    for vr, dr in row_tables:
        rb, win = vr.shape
        b = jnp.take(t, dr.reshape(-1), axis=1).reshape(h, rb, win, w, w)
        valid = vr[:, :, None, None] & vc[None, None]
        b = jnp.where(valid[None], b * LOG2E, MASK_VALUE).transpose(0, 1, 3, 2, 4)
        out.append(b.reshape(h, rb * w, win * w))
    return jnp.stack(out, axis=1).astype(F32)


def _attn_b_kernel(start_ref, cls_ref, o_in_ref, q_ref, kl_ref, vl_ref, kc_ref, vc_ref, bias_ref, o_ref,
                   k_all, v_ext, s_buf, *, nblk, qb, kwin, w):
    del o_in_ref
    _fill_kv_ext(k_all, v_ext, kl_ref, vl_ref, kc_ref, vc_ref)
    n_lat = kl_ref.shape[0]

    def rows(i):
        return pl.ds(pl.multiple_of(i * qb, qb), qb)

    def window(i):
        return pl.ds(pl.multiple_of(start_ref[i] * w, w), kwin)

    def scores(i, carry):
        q = q_ref[rows(i), :]
        s_buf[i, :, :kwin] = _dot_nt(q, k_all[window(i), :]) + bias_ref[0, cls_ref[i]]
        s_buf[i, :, kwin:] = _dot_nt(q, k_all[n_lat:, :])
        return carry

    def outputs(i, carry):
        s = s_buf[i]
        p = jnp.exp2(s - jnp.max(s, axis=-1, keepdims=True)).astype(BF16)
        oe = _dot(p[:, :kwin], v_ext[window(i), :]) + _dot(p[:, kwin:], v_ext[n_lat:, :])
        o_ref[rows(i), :] = (oe[:, :LANE] / oe[:, LANE:]).astype(o_ref.dtype)
        return carry

    lax.fori_loop(0, nblk, scores, 0, unroll=4)
    lax.fori_loop(0, nblk, outputs, 0, unroll=2)


def _attn_b(cfg, tl, p, o, bias, starts, classes, win):
    w = cfg.grid_w
    qb, kwin = tl.rb * w, win * w
    nblk = cfg.rows // tl.rb
    ncls = bias.shape[1]
    ctx_blk0 = cfg.n_lat // cfg.ctx_len
    q0, k0, v0 = cfg.wa_q // LANE, cfg.off_kb // LANE, cfg.off_vb // LANE
    grid_spec = pltpu.PrefetchScalarGridSpec(
        num_scalar_prefetch=2,
        grid=(cfg.heads_b, cfg.batch),
        in_specs=[
            pl.BlockSpec(memory_space=pl.ANY),
            pl.BlockSpec((cfg.seq, LANE), lambda h, b, s, c: (b, q0 + h)),
            pl.BlockSpec((cfg.seq, LANE), lambda h, b, s, c: (b, k0 + h)),
            pl.BlockSpec((cfg.seq, LANE), lambda h, b, s, c: (b, v0 + h)),
            pl.BlockSpec((cfg.ctx_len, LANE), lambda h, b, s, c: (ctx_blk0 + b, k0 + h)),
            pl.BlockSpec((cfg.ctx_len, LANE), lambda h, b, s, c: (ctx_blk0 + b, v0 + h)),
            pl.BlockSpec((1, ncls, qb, kwin), lambda h, b, s, c: (h, 0, 0, 0)),
        ],
        out_specs=pl.BlockSpec((cfg.seq, LANE), lambda h, b, s, c: (b, q0 + h)),
        scratch_shapes=[pltpu.VMEM((cfg.seq + cfg.ctx_len, LANE), BF16),
                        pltpu.VMEM((cfg.seq + cfg.ctx_len, 2 * LANE), BF16),
                        pltpu.VMEM((nblk, qb, kwin + cfg.ctx_len), F32)],
    )
    return pl.pallas_call(
        functools.partial(_attn_b_kernel, nblk=nblk, qb=qb, kwin=kwin, w=w),
        grid_spec=grid_spec,
        out_shape=jax.ShapeDtypeStruct(o.shape, o.dtype),
        input_output_aliases={2: 0},
        compiler_params=_params("parallel", "arbitrary"),
        name="attn_neighbourhood",
    )(jnp.asarray(starts, jnp.int32), jnp.asarray(classes, jnp.int32), o, p, p, p, p, p, bias)


def _attn_ctx_kernel(o_in_ref, p_ref, o_ref, *, cfg):
    del o_in_ref
    group = cfg.heads_a // cfg.kv_a

    def cols(off, h):
        return p_ref[:, off + h * LANE: off + (h + 1) * LANE]

    for h in range(cfg.heads_a):
        g = h // group
        o = _softmax_pv([_dot_nt(cols(0, h), cols(cfg.off_ka, g))], [cols(cfg.off_va, g)])
        o_ref[:, h * LANE:(h + 1) * LANE] = o.astype(o_ref.dtype)
    for h in range(cfg.heads_b):
        o = _softmax_pv([_dot_nt(cols(cfg.wa_q, h), cols(cfg.off_kb, h))], [cols(cfg.off_vb, h)])
        o_ref[:, cfg.wa_q + h * LANE: cfg.wa_q + (h + 1) * LANE] = o.astype(o_ref.dtype)


def _attn_ctx(cfg, p, o):
    ctx_blk0 = cfg.n_lat // cfg.ctx_len
    return pl.pallas_call(
        functools.partial(_attn_ctx_kernel, cfg=cfg),
        grid=(cfg.batch,),
        in_specs=[
            pl.BlockSpec(memory_space=pl.ANY),
            pl.BlockSpec((cfg.ctx_len, cfg.in_cols), lambda b: (ctx_blk0 + b, 0)),
        ],
        out_specs=pl.BlockSpec((cfg.ctx_len, cfg.mix_w), lambda b: (ctx_blk0 + b, 0)),
        out_shape=jax.ShapeDtypeStruct(o.shape, o.dtype),
        input_output_aliases={0: 0},
        compiler_params=_params("parallel"),
        name="attn_context",
    )(o, p)


def _out_proj_kernel(o_ref, w_ref, x_ref, gate_ref, y_ref):
    half = y_ref.shape[1] // 2
    gate = gate_ref[0, 0]
    for c in (slice(0, half), slice(half, 2 * half)):
        y_ref[:, c] = x_ref[:, c] + gate[:, c] * _dot(o_ref[...], w_ref[:, c])


def _out_proj(cfg, tl, o, w_out, layer, x_all, mod, n_rows):
    tm, tn = tl.tm_proj, tl.tn_out
    n_tiles = n_rows // tm
    return pl.pallas_call(
        _out_proj_kernel,
        grid=(n_tiles, cfg.d_model // tn),
        in_specs=[
            pl.BlockSpec((tm, cfg.mix_w), lambda i, j: (i, 0)),
            pl.BlockSpec((None, cfg.mix_w, tn), lambda i, j: (layer, 0, j)),
            pl.BlockSpec((tm, tn), lambda i, j: (i, j)),
            _mod_spec(cfg, tm, 2, tn, 2, col_axis=1),
        ],
        out_specs=pl.BlockSpec((tm, tn), lambda i, j: (i, j)),
        out_shape=jax.ShapeDtypeStruct((n_rows, cfg.d_model), F32),
        compiler_params=_params("parallel", "arbitrary"),
        name="out_proj",
    )(o, w_out, x_all, mod)


def _router_kernel(x_ref, g_ref, shift_ref, scale_ref, wr_ref, h_ref, route_ref, *, n_experts):
    h = _modulated(x_ref[...], g_ref, shift_ref, scale_ref)
    h_ref[...] = h
    logits = jnp.dot(h, wr_ref[...], preferred_element_type=F32, precision=lax.Precision.HIGHEST)
    lane = lax.broadcasted_iota(jnp.int32, logits.shape, 1).astype(F32)
    neg = jnp.float32(-jnp.inf)
    logits = jnp.where(lane < n_experts, logits, neg)
    v1 = jnp.max(logits, axis=-1, keepdims=True)
    i1 = jnp.min(jnp.where(logits == v1, lane, float(LANE)), axis=-1, keepdims=True)
    rest = jnp.where(lane == i1, neg, logits)
    v2 = jnp.max(rest, axis=-1, keepdims=True)
    i2 = jnp.min(jnp.where(rest == v2, lane, float(LANE)), axis=-1, keepdims=True)
    w1 = 1.0 / (1.0 + jnp.exp(v2 - v1))
    route = jnp.where(lane == 0.0, w1, jnp.where(lane == 1.0, 1.0 - w1, jnp.where(lane == 2.0, i1, i2)))
    route_ref[...] = jnp.where(lane < 4.0, route, 0.0)


def _router(cfg, tl, x_all, g, mod, w_router_pad, n_tiles):
    tm, d = tl.tm, cfg.d_model
    return pl.pallas_call(
        functools.partial(_router_kernel, n_experts=cfg.n_experts),
        grid=(n_tiles,),
        in_specs=[
            pl.BlockSpec((tm, d), lambda i: (i, 0)),
            pl.BlockSpec((1, d), lambda i: (0, 0)),
            _mod_spec(cfg, tm, 3, d, 1),
            _mod_spec(cfg, tm, 4, d, 1),
            pl.BlockSpec((d, LANE), lambda i: (0, 0)),
        ],
        out_specs=[pl.BlockSpec((tm, d), lambda i: (i, 0)), pl.BlockSpec((tm, LANE), lambda i: (i, 0))],
        out_shape=[jax.ShapeDtypeStruct((n_tiles * tm, d), F32), jax.ShapeDtypeStruct((n_tiles * tm, LANE), F32)],
        compiler_params=_params("parallel"),
        name="router",
    )(x_all, g, mod, mod, w_router_pad)


def _swiglu_step(h, w1_ref, w3_ref, w2_ref):
    a = _dot(h, w1_ref[0])
    b = _dot(h, w3_ref[0])
    return _dot((a * jax.nn.sigmoid(a) * b).astype(BF16), w2_ref[0])


def _ffn_kernel(x_ref, g_ref, shift_ref, scale_ref, gate_ref, w1_ref, w3_ref, w2_ref, y_ref, h_ref):
    f = pl.program_id(1)

    @pl.when(f == 0)
    def _():
        h_ref[...] = _modulated(x_ref[...], g_ref, shift_ref, scale_ref).astype(BF16)
        y_ref[...] = jnp.zeros_like(y_ref)

    y_ref[...] += _swiglu_step(h_ref[...], w1_ref, w3_ref, w2_ref)

    @pl.when(f == pl.num_programs(1) - 1)
    def _():
        y_ref[...] = x_ref[...] + gate_ref[0, 0] * y_ref[...]


def _ffn(cfg, tl, x_all, g, mod, w1, w3, w2, layer, n_tiles):
    tm, tf, d = tl.tm, tl.tf, cfg.d_model
    return pl.pallas_call(
        _ffn_kernel,
        grid=(n_tiles, cfg.d_ff // tf),
        in_specs=[
            pl.BlockSpec((tm, d), lambda i, f: (i, 0)),
            pl.BlockSpec((1, d), lambda i, f: (0, 0)),
            _mod_spec(cfg, tm, 3, d, 2),
            _mod_spec(cfg, tm, 4, d, 2),
            _mod_spec(cfg, tm, 5, d, 2),
            pl.BlockSpec((1, d, tf), lambda i, f: (layer, 0, f)),
            pl.BlockSpec((1, d, tf), lambda i, f: (layer, 0, f)),
            pl.BlockSpec((1, tf, d), lambda i, f: (layer, f, 0)),
        ],
        out_specs=pl.BlockSpec((tm, d), lambda i, f: (i, 0)),
        out_shape=jax.ShapeDtypeStruct((n_tiles * tm, d), F32),
        scratch_shapes=[pltpu.VMEM((tm, d), BF16)],
        compiler_params=_params("parallel", "arbitrary"),
        name="ffn_dense",
    )(x_all, g, mod, mod, mod, w1, w3, w2)


def _route_plan(cfg, tm, tm_pad, route, n_rows):
    n_e = cfg.n_experts
    n_assign = 2 * n_rows
    n_tiles_max = n_assign // tm + n_e + 1
    expert = route[:, 2:4].astype(jnp.int32).reshape(n_assign)
    onehot = (expert[:, None] == jnp.arange(n_e)[None, :]).astype(jnp.int32)
    csum = jnp.cumsum(onehot, axis=0)
    rank = jnp.sum(csum * onehot, axis=1) - 1
    counts = csum[-1]
    tiles_per_e = (counts + tm - 1) // tm
    tile_end = jnp.cumsum(tiles_per_e)
    tile_start = tile_end - tiles_per_e
    n_used = tile_end[-1]
    pos = jnp.sum(onehot * tile_start[None, :], axis=1) * tm + rank
    tile_ids = jnp.arange(n_tiles_max)
    tile_expert = jnp.minimum(jnp.sum(tile_ids[:, None] >= tile_end[None, :], axis=1), n_e - 1)
    tile_expert = jnp.where(tile_ids < n_used, tile_expert, tile_expert[jnp.maximum(n_used - 1, 0)])
    order = jnp.argsort(expert, stable=True)
    unpadded_start = jnp.cumsum(counts) - counts
    k = (tile_ids - tile_start[tile_expert])[:, None] * tm + jnp.arange(tm)[None, :]
    valid = jnp.logical_and(k < counts[tile_expert][:, None], (tile_ids < n_used)[:, None])
    src = order[jnp.clip(unpadded_start[tile_expert][:, None] + k, 0, n_assign - 1)] // 2
    src_tok = jnp.where(valid, src, 0).astype(jnp.int32)
    src_tok = jnp.pad(src_tok, ((0, 0), (0, tm_pad - tm))).reshape(n_tiles_max, 1, tm_pad)
    pos = pos.astype(jnp.int32).reshape(n_rows // tm, 1, tm, 2)
    return (tile_expert.astype(jnp.int32), n_used.astype(jnp.int32).reshape(1), src_tok,
            pos[..., 0], pos[..., 1])


def _start_row_gather(idx_ref, src_hbm, dst_ref, sem, n_rows):
    def body(r, carry):
        pltpu.make_async_copy(src_hbm.at[pl.ds(idx_ref[0, 0, r], 1), :], dst_ref.at[pl.ds(r, 1), :], sem).start()
        return carry
    lax.fori_loop(0, n_rows, body, 0, unroll=8)


def _wait_row_gather(src_hbm, dst_ref, sem, n_rows):
    def body(r, carry):
        pltpu.make_async_copy(src_hbm.at[pl.ds(0, 1), :], dst_ref.at[pl.ds(r, 1), :], sem).wait()
        return carry
    lax.fori_loop(0, n_rows, body, 0, unroll=8)


def _moe_ffn_kernel(te_ref, nu_ref, src0_ref, src1_ref, h_hbm, w1_ref, w3_ref, w2_ref, y_ref,
                    hbuf, h_ref, sem, *, tm, rows_per_step):
    del te_ref
    t, f = pl.program_id(0), pl.program_id(1)
    n_used = nu_ref[0]
    slot = lax.rem(t, 2)
    live = t < n_used
    at_tile_start = f == 0
    tm_pad = hbuf.shape[1]

    @pl.when(jnp.logical_and(at_tile_start, t == 0))
    def _():
        _start_row_gather(src0_ref, h_hbm, hbuf.at[0], sem.at[0], tm_pad)

    @pl.when(jnp.logical_and(at_tile_start, t <= n_used))
    def _():
        _wait_row_gather(h_hbm, hbuf.at[slot], sem.at[slot], tm_pad)

    @pl.when(jnp.logical_and(at_tile_start, live))
    def _():
        h_ref[...] = hbuf[slot, :tm].astype(BF16)
        y_ref[...] = jnp.zeros_like(y_ref)

    @pl.when(live)
    def _():
        for k in range(rows_per_step):
            r = f * rows_per_step + k
            pltpu.make_async_copy(h_hbm.at[pl.ds(src1_ref[0, 0, r], 1), :],
                                  hbuf.at[1 - slot, pl.ds(r, 1), :], sem.at[1 - slot]).start()
        y_ref[...] += _swiglu_step(h_ref[...], w1_ref, w3_ref, w2_ref)


def _moe_ffn(cfg, tl, h, w1, w3, w2, layer, tile_expert, n_used, src_tok):
    tm, tf, d = tl.tm, tl.tf, cfg.d_model
    n_tiles_max, _, tm_pad = src_tok.shape
    nf = cfg.d_ff // tf
    assert tm_pad % nf == 0 and tm_pad >= tm

    def live_tile(t, nu):
        return jnp.minimum(t, nu[0] - 1)

    def f_idx(t, f, nu):
        return jnp.where(t < nu[0], f, nf - 1)

    grid_spec = pltpu.PrefetchScalarGridSpec(
        num_scalar_prefetch=2,
        grid=(n_tiles_max, nf),
        in_specs=[
            pl.BlockSpec((1, 1, tm_pad), lambda t, f, te, nu: (0, 0, 0), memory_space=pltpu.SMEM),
            pl.BlockSpec((1, 1, tm_pad), lambda t, f, te, nu: (live_tile(t + 1, nu), 0, 0),
                         memory_space=pltpu.SMEM),
            pl.BlockSpec(memory_space=pl.ANY),
            pl.BlockSpec((None, 1, d, tf), lambda t, f, te, nu: (layer, te[t], 0, f_idx(t, f, nu))),
            pl.BlockSpec((None, 1, d, tf), lambda t, f, te, nu: (layer, te[t], 0, f_idx(t, f, nu))),
            pl.BlockSpec((None, 1, tf, d), lambda t, f, te, nu: (layer, te[t], f_idx(t, f, nu), 0)),
        ],
        out_specs=pl.BlockSpec((tm, d), lambda t, f, te, nu: (live_tile(t, nu), 0)),
        scratch_shapes=[pltpu.VMEM((2, tm_pad, d), F32), pltpu.VMEM((tm, d), BF16),
                        pltpu.SemaphoreType.DMA((2,))],
    )
    return pl.pallas_call(
        functools.partial(_moe_ffn_kernel, tm=tm, rows_per_step=tm_pad // nf),
        grid_spec=grid_spec,
        out_shape=jax.ShapeDtypeStruct((n_tiles_max * tm, d), F32),
        compiler_params=_params("arbitrary", "arbitrary"),
        name="moe_ffn",
    )(tile_expert, n_used, src_tok, src_tok, h, w1, w3, w2)


def _moe_combine_kernel(pa0_ref, pb0_ref, pa1_ref, pb1_ref, y_hbm, x_ref, gate_ref, route_ref, o_ref,
                        ya, yb, sem, *, tm):
    i = pl.program_id(0)
    slot = lax.rem(i, 2)

    def start(pa_ref, pb_ref, s):
        _start_row_gather(pa_ref, y_hbm, ya.at[s], sem.at[s], tm)
        _start_row_gather(pb_ref, y_hbm, yb.at[s], sem.at[s], tm)

    @pl.when(i == 0)
    def _():
        start(pa0_ref, pb0_ref, 0)

    @pl.when(i + 1 < pl.num_programs(0))
    def _():
        start(pa1_ref, pb1_ref, 1 - slot)

    _wait_row_gather(y_hbm, ya.at[slot], sem.at[slot], tm)
    _wait_row_gather(y_hbm, yb.at[slot], sem.at[slot], tm)
    route = route_ref[...]
    y = route[:, 0:1] * ya[slot] + route[:, 1:2] * yb[slot]
    o_ref[...] = x_ref[...] + gate_ref[0, 0] * y


def _moe_combine(cfg, tl, y_sorted, x_all, mod, route, pos_a, pos_b, n_tiles):
    tm, d = tl.tm, cfg.d_model
    first = pl.BlockSpec((1, 1, tm), lambda i: (0, 0, 0), memory_space=pltpu.SMEM)
    nxt = pl.BlockSpec((1, 1, tm), lambda i: (jnp.minimum(i + 1, n_tiles - 1), 0, 0), memory_space=pltpu.SMEM)
    return pl.pallas_call(
        functools.partial(_moe_combine_kernel, tm=tm),
        grid=(n_tiles,),
        in_specs=[
            first, first, nxt, nxt,
            pl.BlockSpec(memory_space=pl.ANY),
            pl.BlockSpec((tm, d), lambda i: (i, 0)),
            _mod_spec(cfg, tm, 5, d, 1),
            pl.BlockSpec((tm, LANE), lambda i: (i, 0)),
        ],
        out_specs=pl.BlockSpec((tm, d), lambda i: (i, 0)),
        out_shape=jax.ShapeDtypeStruct((n_tiles * tm, d), F32),
        scratch_shapes=[pltpu.VMEM((2, tm, d), F32), pltpu.VMEM((2, tm, d), F32), pltpu.SemaphoreType.DMA((2,))],
        compiler_params=_params("arbitrary"),
        name="moe_combine",
    )(pos_a, pos_b, pos_a, pos_b, y_sorted, x_all, mod, route)


def _rope_tables(cfg, tm):
    half = cfg.head_dim // 2
    quarter = half // 2
    t = jnp.arange(cfg.seq)
    freqs = ROPE_THETA ** (-jnp.arange(quarter, dtype=F32) / quarter)
    ang_r = (t // cfg.grid_w).astype(F32)[:, None] * freqs[None, :]
    ang_c = (t % cfg.grid_w).astype(F32)[:, None] * freqs[None, :]
    cos = jnp.concatenate([jnp.cos(ang_r)] * 2 + [jnp.cos(ang_c)] * 2, axis=-1)
    sin = jnp.concatenate([-jnp.sin(ang_r), jnp.sin(ang_r), -jnp.sin(ang_c), jnp.sin(ang_c)], axis=-1)
    cos = jnp.concatenate([cos, jnp.ones((tm, cfg.head_dim), F32)], axis=0)
    sin = jnp.concatenate([sin, jnp.zeros((tm, cfg.head_dim), F32)], axis=0)
    return cos, sin


def _forward(cfg, x, c, ctx, c_ctx, ada_w, ada_b, norm_attn, norm_ffn, w_in, qn_a, kn_a, qn_b, kn_b, rpb,
             w_out, w1_dense, w3_dense, w2_dense, w_router, w1_moe, w3_moe, w2_moe):
    tl = _pick_tiles(cfg)
    d = cfg.d_model
    n_tiles = cfg.n_tok // tl.tm
    n_lat_tiles = cfg.n_lat // tl.tm

    x_all = jnp.concatenate([x.reshape(cfg.n_lat, d), ctx.reshape(cfg.batch * cfg.ctx_len, d)], axis=0)

    cond = jnp.concatenate([c, c_ctx[None, :]], axis=0)
    pad_rows = -(-cond.shape[0] // 16) * 16
    cond_pad = jnp.zeros((pad_rows, d), F32).at[:cond.shape[0]].set(cond)
    mod_all = _ada_mod(cfg, tl, cond_pad, ada_w, ada_b)

    rope_cos, rope_sin = _rope_tables(cfg, tl.tm_proj)
    win, starts, classes, row_tables, col_table = _na_geometry(cfg, tl.rb)
    q_scale = cfg.head_dim ** -0.5 * LOG2E
    ones_a = jnp.ones((cfg.wa_kv,), F32)
    ones_b = jnp.ones((cfg.wb,), F32)
    w_in, w_out = w_in.astype(BF16), w_out.astype(BF16)
    w1_dense, w3_dense, w2_dense = (w.astype(BF16) for w in (w1_dense, w3_dense, w2_dense))
    w1_moe, w3_moe, w2_moe = (w.astype(BF16) for w in (w1_moe, w3_moe, w2_moe))
    nf = cfg.d_ff // tl.tf
    tm_pad = -(-tl.tm // (8 * nf)) * 8 * nf

    for i in range(cfg.depth):
        last = i == cfg.depth - 1
        j = i // 2
        mod = mod_all[i, :cfg.batch + 1].reshape(cfg.batch + 1, N_MOD, 1, d)
        gain = jnp.concatenate([
            jnp.tile(qn_a[i], cfg.heads_a) * q_scale, jnp.tile(qn_b[i], cfg.heads_b) * q_scale,
            jnp.tile(kn_a[i], cfg.kv_a), ones_a, jnp.tile(kn_b[i], cfg.heads_b), ones_b])[None, :]

        p = _in_proj(cfg, tl, x_all, norm_attn[i][None, :], mod, w_in, i, gain, rope_cos, rope_sin)
        o = _attn_a(cfg, tl, p)
        o = _attn_b(cfg, tl, p, o, _na_bias(rpb[i], row_tables, col_table), starts, classes, win)
        if not last:
            o = _attn_ctx(cfg, p, o)
        live_tiles = n_lat_tiles if last else n_tiles
        x_all = _out_proj(cfg, tl, o, w_out, i, x_all, mod, live_tiles * tl.tm)

        g_ffn = norm_ffn[i][None, :]
        if i % 2 == 0:
            x_all = _ffn(cfg, tl, x_all, g_ffn, mod, w1_dense, w3_dense, w2_dense, j, live_tiles)
        else:
            wr = jnp.zeros((d, LANE), F32).at[:, :cfg.n_experts].set(w_router[j])
            h, route = _router(cfg, tl, x_all, g_ffn, mod, wr, live_tiles)
            tile_expert, n_used, src_tok, pos_a, pos_b = _route_plan(cfg, tl.tm, tm_pad, route,
                                                                     live_tiles * tl.tm)
            y_sorted = _moe_ffn(cfg, tl, h, w1_moe, w3_moe, w2_moe, j, tile_expert, n_used, src_tok)
            x_all = _moe_combine(cfg, tl, y_sorted, x_all, mod, route, pos_a, pos_b, live_tiles)

    return x_all.reshape(cfg.batch, cfg.seq, d)


def kernel(x, c, ctx, c_ctx, ada_w, ada_b, norm_attn, norm_ffn, w_in, qn_a, kn_a, qn_b, kn_b, rpb, w_out,
           w1_dense, w3_dense, w2_dense, w_router, w1_moe, w3_moe, w2_moe):
    batch, seq, d_model = x.shape
    depth = w_in.shape[0]
    head_dim = qn_a.shape[-1]
    heads_b = rpb.shape[1]
    wb = heads_b * head_dim
    wa_q = w_out.shape[1] - wb
    wa_kv = (w_in.shape[2] - wa_q - 3 * wb) // 2
    cfg = Cfg(d_model=d_model, batch=batch, seq=seq, ctx_len=ctx.shape[1], grid_w=64, head_dim=head_dim,
              heads_a=wa_q // head_dim, kv_a=wa_kv // head_dim, heads_b=heads_b,
              na_kh=(rpb.shape[2] + 1) // 2, na_kw=(rpb.shape[3] + 1) // 2, d_ff=w1_dense.shape[-1],
              n_experts=w_router.shape[-1], depth=depth)
    return _forward(cfg, x, c, ctx, c_ctx, ada_w, ada_b, norm_attn, norm_ffn, w_in, qn_a, kn_a, qn_b, kn_b,
                    rpb, w_out, w1_dense, w3_dense, w2_dense, w_router, w1_moe, w3_moe, w2_moe)
```

```python
import functools
from typing import NamedTuple

import numpy as np
import jax
import jax.numpy as jnp
from jax import lax
from jax.experimental import pallas as pl
from jax.experimental.pallas import tpu as pltpu

F32 = jnp.float32
BF16 = jnp.bfloat16

EPS = 1e-6
ROPE_THETA = 10000.0
N_MOD = 6
LANE = 128
MASK_VALUE = -1e30
LOG2E = 1.4426950408889634
V7X_VMEM_LIMIT = 56 * 1024 * 1024


class Cfg(NamedTuple):
    d_model: int
    batch: int
    seq: int
    ctx_len: int
    grid_w: int
    head_dim: int
    heads_a: int
    kv_a: int
    heads_b: int
    na_kh: int
    na_kw: int
    d_ff: int
    n_experts: int
    depth: int

    @property
    def wa_q(self):
        return self.heads_a * self.head_dim

    @property
    def wa_kv(self):
        return self.kv_a * self.head_dim

    @property
    def wb(self):
        return self.heads_b * self.head_dim

    @property
    def mix_w(self):
        return self.wa_q + self.wb

    @property
    def in_cols(self):
        return self.mix_w + 2 * self.wa_kv + 2 * self.wb

    @property
    def rows(self):
        return self.seq // self.grid_w

    @property
    def n_lat(self):
        return self.batch * self.seq

    @property
    def n_tok(self):
        return self.batch * (self.seq + self.ctx_len)

    @property
    def off_ka(self):
        return self.mix_w

    @property
    def off_va(self):
        return self.mix_w + self.wa_kv

    @property
    def off_kb(self):
        return self.mix_w + 2 * self.wa_kv

    @property
    def off_vb(self):
        return self.mix_w + 2 * self.wa_kv + self.wb


class Tiles(NamedTuple):
    tm: int
    tm_proj: int
    hps: int
    tn_out: int
    tf: int
    tq: int
    rb: int
    tn_ada: int


def _pick_tiles(cfg):
    n_ctx = cfg.batch * cfg.ctx_len
    tm = min(512, cfg.seq, n_ctx)
    tm_proj = min(1024, cfg.seq, n_ctx)
    for t in (tm, tm_proj):
        assert cfg.seq % t == 0 and n_ctx % t == 0
    n_heads = cfg.in_cols // LANE
    hps = next(h for h in (12, 10, 8, 6, 4, 2) if n_heads % h == 0)
    tn_out = min(1024, cfg.d_model)
    tf = min(512, cfg.d_ff)
    assert cfg.d_model % tn_out == 0 and cfg.d_ff % tf == 0
    tq = min(256, cfg.seq)
    assert cfg.seq % tq == 0
    rb = 4
    assert cfg.rows % (2 * rb) == 0 and cfg.rows >= rb + cfg.na_kh
    tn_ada = next(t for t in (1024, 512, 256, LANE) if (N_MOD * cfg.d_model) % t == 0)
    return Tiles(tm, tm_proj, hps, tn_out, tf, tq, rb, tn_ada)


def _params(*sem):
    return pltpu.CompilerParams(dimension_semantics=sem, vmem_limit_bytes=V7X_VMEM_LIMIT)


def _dot(a, b):
    return jnp.dot(a, b, preferred_element_type=F32)


def _dot_nt(a, b):
    return lax.dot_general(a, b, (((1,), (1,)), ((), ())), preferred_element_type=F32)


def _rms(x):
    return x * lax.rsqrt(jnp.mean(x * x, axis=-1, keepdims=True) + EPS)


def _ada_kernel(cond_ref, w_ref, b_ref, o_ref):
    c = cond_ref[...]
    c = (c * jax.nn.sigmoid(c)).astype(BF16)
    o_ref[0] = _dot(c, w_ref[0].astype(BF16)) + b_ref[0]


def _ada_mod(cfg, tl, cond_pad, ada_w, ada_b):
    depth, d, n6 = ada_w.shape
    mp = cond_pad.shape[0]
    return pl.pallas_call(
        _ada_kernel,
        grid=(depth, n6 // tl.tn_ada),
        in_specs=[
            pl.BlockSpec((mp, d), lambda l, j: (0, 0)),
            pl.BlockSpec((1, d, tl.tn_ada), lambda l, j: (l, 0, j)),
            pl.BlockSpec((1, 1, tl.tn_ada), lambda l, j: (l, 0, j)),
        ],
        out_specs=pl.BlockSpec((1, mp, tl.tn_ada), lambda l, j: (l, 0, j)),
        out_shape=jax.ShapeDtypeStruct((depth, mp, n6), F32),
        compiler_params=_params("parallel", "parallel"),
        name="ada_mod",
    )(cond_pad, ada_w, ada_b.reshape(depth, 1, n6))


def _mod_index(cfg, tm):
    n_lat_tiles = cfg.n_lat // tm
    per_sample = cfg.seq // tm

    def f(i):
        return jnp.where(i < n_lat_tiles, i // per_sample, cfg.batch)
    return f


def _mod_spec(cfg, tm, k, ncols, grid_rank, col_axis=None):
    mi = _mod_index(cfg, tm)
    if col_axis is None:
        return pl.BlockSpec((1, 1, 1, ncols), lambda i, *rest: (mi(i), k, 0, 0))
    assert grid_rank == 2 and col_axis == 1
    return pl.BlockSpec((1, 1, 1, ncols), lambda i, j: (mi(i), k, 0, j))


def _modulated(x, g_ref, shift_ref, scale_ref):
    y = _rms(x) * g_ref[...]
    return y * (1.0 + scale_ref[0, 0]) + shift_ref[0, 0]


def _rotary_partner(y):
    return pltpu.roll(y, LANE // 2, 1)


def _rotary_layout(a, axis):
    shape = a.shape
    quarter = shape[axis] // 4
    a = a.reshape(shape[:axis] + (2, 2, quarter) + shape[axis + 1:])
    return jnp.swapaxes(a, axis, axis + 1).reshape(shape)


HEAD_ROPE, HEAD_NORM, HEAD_PLAIN = "rope", "norm", "plain"


def _proj_kernel(x_ref, g_ref, shift_ref, scale_ref, w_ref, gain_ref, cos_ref, sin_ref, o_ref, h_ref,
                 *, step_kinds):
    j = pl.program_id(1)

    @pl.when(j == 0)
    def _():
        h_ref[...] = _modulated(x_ref[...], g_ref, shift_ref, scale_ref).astype(BF16)

    def head_epilogue(y, kind, k):
        if kind != HEAD_PLAIN:
            y = _rms(y) * gain_ref[:, k * LANE:(k + 1) * LANE]
        if kind == HEAD_ROPE:
            y = y * cos_ref[...] + _rotary_partner(y) * sin_ref[...]
        return y

    def run(kinds):
        for c in range(0, len(kinds), 2):
            acc = _dot(h_ref[...], w_ref[:, c * LANE:(c + 2) * LANE])
            for k in (c, c + 1):
                y = head_epilogue(acc[:, (k - c) * LANE:(k - c + 1) * LANE], kinds[k], k)
                o_ref[:, k * LANE:(k + 1) * LANE] = y.astype(o_ref.dtype)

    for kinds in sorted(set(step_kinds)):
        steps = [s for s, kk in enumerate(step_kinds) if kk == kinds]
        cond = functools.reduce(jnp.logical_or, [j == s for s in steps])
        pl.when(cond)(functools.partial(run, kinds))


def _in_proj(cfg, tl, x_all, g, mod, w_in, layer, gain, rope_cos, rope_sin):
    tm, tn = tl.tm_proj, tl.hps * LANE
    d = cfg.d_model
    n_tiles = cfg.n_tok // tm
    n_lat_tiles = cfg.n_lat // tm
    per_sample = cfg.seq // tm
    head_kinds = ([HEAD_ROPE] * cfg.heads_a + [HEAD_NORM] * cfg.heads_b + [HEAD_ROPE] * cfg.kv_a
                  + [HEAD_PLAIN] * cfg.kv_a + [HEAD_NORM] * cfg.heads_b + [HEAD_PLAIN] * cfg.heads_b)
    step_kinds = tuple(tuple(head_kinds[s:s + tl.hps]) for s in range(0, len(head_kinds), tl.hps))
    kern = functools.partial(_proj_kernel, step_kinds=step_kinds)

    def rope_idx(i, j):
        return (jnp.where(i < n_lat_tiles, i % per_sample, per_sample), 0)

    return pl.pallas_call(
        kern,
        grid=(n_tiles, cfg.in_cols // tn),
        in_specs=[
            pl.BlockSpec((tm, d), lambda i, j: (i, 0)),
            pl.BlockSpec((1, d), lambda i, j: (0, 0)),
            _mod_spec(cfg, tm, 0, d, 2),
            _mod_spec(cfg, tm, 1, d, 2),
            pl.BlockSpec((None, d, tn), lambda i, j: (layer, 0, j)),
            pl.BlockSpec((1, tn), lambda i, j: (0, j)),
            pl.BlockSpec((tm, LANE), rope_idx),
            pl.BlockSpec((tm, LANE), rope_idx),
        ],
        out_specs=pl.BlockSpec((tm, tn), lambda i, j: (i, j)),
        out_shape=jax.ShapeDtypeStruct((cfg.n_tok, cfg.in_cols), BF16),
        scratch_shapes=[pltpu.VMEM((tm, d), BF16)],
        compiler_params=_params("parallel", "arbitrary"),
        name="in_proj",
    )(x_all, g, mod, mod, w_in, gain, rope_cos, rope_sin)


def _softmax_pv(s_parts, v_parts):
    m = None
    for s in s_parts:
        mx = jnp.max(s, axis=-1, keepdims=True)
        m = mx if m is None else jnp.maximum(m, mx)
    l = None
    o = None
    for s, v in zip(s_parts, v_parts):
        p = jnp.exp2(s - m)
        ps = jnp.sum(p, axis=-1, keepdims=True)
        pv = _dot(p.astype(BF16), v)
        l = ps if l is None else l + ps
        o = pv if o is None else o + pv
    return o / l


def _softmax_pv_ext(s_parts, vext_parts):
    m = None
    for s in s_parts:
        mx = jnp.max(s, axis=-1, keepdims=True)
        m = mx if m is None else jnp.maximum(m, mx)
    oe = None
    for s, v in zip(s_parts, vext_parts):
        pv = _dot(jnp.exp2(s - m).astype(BF16), v)
        oe = pv if oe is None else oe + pv
    return oe[:, :LANE] / oe[:, LANE:]


def _softmax_pv_given_max(s, m, v_ext):
    oe = _dot(jnp.exp2(s - m).astype(BF16), v_ext)
    return oe[:, :LANE] / oe[:, LANE:]


def _fill_kv_ext(k_all, v_ext, kl_ref, vl_ref, kc_ref, vc_ref):
    n_lat = kl_ref.shape[0]
    k_all[:n_lat, :] = kl_ref[...]
    k_all[n_lat:, :] = kc_ref[...]
    v_ext[:n_lat, :LANE] = vl_ref[...]
    v_ext[n_lat:, :LANE] = vc_ref[...]
    v_ext[:, LANE:] = jnp.ones((v_ext.shape[0], LANE), v_ext.dtype)


def _attn_a_kernel(q_ref, kl_ref, vl_ref, kc_ref, vc_ref, o_ref, k_all, v_ext, s_buf, m_buf, *, group):
    tq = q_ref.shape[0]
    phase = pl.program_id(3)

    @pl.when(jnp.logical_and(pl.program_id(2) == 0, phase == 0))
    def _():
        _fill_kv_ext(k_all, v_ext, kl_ref, vl_ref, kc_ref, vc_ref)

    @pl.when(phase == 0)
    def _():
        q = jnp.concatenate([q_ref[:, h * LANE:(h + 1) * LANE] for h in range(group)], axis=0)
        s = _dot_nt(q, k_all[...])
        s_buf[...] = s
        m_buf[...] = jnp.max(s, axis=-1, keepdims=True)

    @pl.when(phase == 1)
    def _():
        o = _softmax_pv_given_max(s_buf[...], m_buf[...], v_ext[...])
        for h in range(group):
            o_ref[:, h * LANE:(h + 1) * LANE] = o[h * tq:(h + 1) * tq].astype(o_ref.dtype)


def _attn_a(cfg, tl, p):
    assert cfg.head_dim == LANE
    group = cfg.heads_a // cfg.kv_a
    gw = group * LANE
    tq = tl.tq
    nq = cfg.seq // tq
    ctx_blk0 = cfg.n_lat // cfg.ctx_len
    ka0, va0 = cfg.off_ka // LANE, cfg.off_va // LANE
    n_keys = cfg.seq + cfg.ctx_len
    return pl.pallas_call(
        functools.partial(_attn_a_kernel, group=group),
        grid=(cfg.batch, cfg.kv_a, nq, 2),
        scratch_shapes=[pltpu.VMEM((n_keys, LANE), BF16), pltpu.VMEM((n_keys, 2 * LANE), BF16),
                        pltpu.VMEM((group * tq, n_keys), F32), pltpu.VMEM((group * tq, 1), F32)],
        in_specs=[
            pl.BlockSpec((tq, gw), lambda b, g, i, ph: (b * nq + i, g)),
            pl.BlockSpec((cfg.seq, LANE), lambda b, g, i, ph: (b, ka0 + g)),
            pl.BlockSpec((cfg.seq, LANE), lambda b, g, i, ph: (b, va0 + g)),
            pl.BlockSpec((cfg.ctx_len, LANE), lambda b, g, i, ph: (ctx_blk0 + b, ka0 + g)),
            pl.BlockSpec((cfg.ctx_len, LANE), lambda b, g, i, ph: (ctx_blk0 + b, va0 + g)),
        ],
        out_specs=pl.BlockSpec((tq, gw), lambda b, g, i, ph: (b * nq + i, g)),
        out_shape=jax.ShapeDtypeStruct((cfg.n_tok, cfg.mix_w), BF16),
        compiler_params=_params("arbitrary", "arbitrary", "arbitrary", "arbitrary"),
        name="attn_global",
    )(p, p, p, p, p)


def _na_geometry(cfg, rb):
    rows, w = cfg.rows, cfg.grid_w
    kh = min(cfg.na_kh, rows)
    kw = cfg.na_kw
    win = min(rows, -(-(rb - 1 + kh) // rb) * rb)
    nblk = rows // rb
    r0 = np.clip(np.arange(rows) - kh // 2, 0, rows - kh)
    c0 = np.clip(np.arange(w) - kw // 2, 0, w - kw)
    starts = np.clip(rb * np.arange(nblk) - kh // 2, 0, rows - win)
    col = np.arange(w)
    vc = (col[None, :] >= c0[:, None]) & (col[None, :] < c0[:, None] + kw)
    dc = np.where(vc, col[None, :] - col[:, None] + (cfg.na_kw - 1), 0)

    def row_table(i):
        r = rb * i + np.arange(rb)
        key_r = starts[i] + np.arange(win)
        vr = (key_r[None, :] >= r0[r][:, None]) & (key_r[None, :] < r0[r][:, None] + kh)
        assert (vr.sum(axis=1) == kh).all()
        return vr, np.where(vr, key_r[None, :] - r[:, None] + (cfg.na_kh - 1), 0)

    row_tables, classes = [], []
    for i in range(nblk):
        t = row_table(i)
        for c, u in enumerate(row_tables):
            if all(np.array_equal(a, b) for a, b in zip(t, u)):
                classes.append(c)
                break
        else:
            classes.append(len(row_tables))
            row_tables.append(t)
    return win, starts, np.asarray(classes), row_tables, (vc, dc)


def _na_bias(rpb, row_tables, col_table):
    depth, h, n_dr, n_dc = rpb.shape
    vc, dc = col_table
    w = vc.shape[0]
    onehot = (dc[None] == np.arange(n_dc)[:, None, None]).astype(np.float32)
    t = jnp.einsum("lhdk,kqc->lhdqc", rpb, onehot, precision=lax.Precision.HIGHEST)
    vr = np.stack([v for v, _ in row_tables])
    dr = np.stack([d for _, d in row_tables])
    ncls, rb, win = vr.shape
    b = jnp.take(t, dr.reshape(-1), axis=2).reshape(depth, h, ncls, rb, win, w, w)
    valid = vr[:, :, :, None, None] & vc[None, None, None]
    b = jnp.where(valid[None, None], b * LOG2E, MASK_VALUE).transpose(0, 1, 2, 3, 5, 4, 6)
    return b.reshape(depth, h, ncls, rb * w, win * w).astype(F32)


def _attn_b_kernel(start_ref, cls_ref, o_in_ref, q_ref, kl_ref, vl_ref, kc_ref, vc_ref, bias_ref, o_ref,
                   k_all, v_ext, s_buf, *, nblk, qb, kwin, w):
    del o_in_ref
    _fill_kv_ext(k_all, v_ext, kl_ref, vl_ref, kc_ref, vc_ref)
    n_lat = kl_ref.shape[0]

    def rows(i):
        return pl.ds(pl.multiple_of(i * qb, qb), qb)

    def window(i):
        return pl.ds(pl.multiple_of(start_ref[i] * w, w), kwin)

    def scores(i, carry):
        q = q_ref[rows(i), :]
        s_buf[i, :, :kwin] = _dot_nt(q, k_all[window(i), :]) + bias_ref[0, cls_ref[i]]
        s_buf[i, :, kwin:] = _dot_nt(q, k_all[n_lat:, :])
        return carry

    def outputs(i, carry):
        s = s_buf[i]
        p = jnp.exp2(s - jnp.max(s, axis=-1, keepdims=True)).astype(BF16)
        oe = _dot(p[:, :kwin], v_ext[window(i), :]) + _dot(p[:, kwin:], v_ext[n_lat:, :])
        o_ref[rows(i), :] = (oe[:, :LANE] / oe[:, LANE:]).astype(o_ref.dtype)
        return carry

    lax.fori_loop(0, nblk, scores, 0, unroll=4)
    lax.fori_loop(0, nblk, outputs, 0, unroll=4)


def _attn_b(cfg, tl, p, o, bias, layer, starts, classes, win):
    w = cfg.grid_w
    qb, kwin = tl.rb * w, win * w
    nblk = cfg.rows // tl.rb
    ncls = bias.shape[2]
    ctx_blk0 = cfg.n_lat // cfg.ctx_len
    q0, k0, v0 = cfg.wa_q // LANE, cfg.off_kb // LANE, cfg.off_vb // LANE
    grid_spec = pltpu.PrefetchScalarGridSpec(
        num_scalar_prefetch=2,
        grid=(cfg.heads_b, cfg.batch),
        in_specs=[
            pl.BlockSpec(memory_space=pl.ANY),
            pl.BlockSpec((cfg.seq, LANE), lambda h, b, s, c: (b, q0 + h)),
            pl.BlockSpec((cfg.seq, LANE), lambda h, b, s, c: (b, k0 + h)),
            pl.BlockSpec((cfg.seq, LANE), lambda h, b, s, c: (b, v0 + h)),
            pl.BlockSpec((cfg.ctx_len, LANE), lambda h, b, s, c: (ctx_blk0 + b, k0 + h)),
            pl.BlockSpec((cfg.ctx_len, LANE), lambda h, b, s, c: (ctx_blk0 + b, v0 + h)),
            pl.BlockSpec((None, 1, ncls, qb, kwin), lambda h, b, s, c: (layer, h, 0, 0, 0)),
        ],
        out_specs=pl.BlockSpec((cfg.seq, LANE), lambda h, b, s, c: (b, q0 + h)),
        scratch_shapes=[pltpu.VMEM((cfg.seq + cfg.ctx_len, LANE), BF16),
                        pltpu.VMEM((cfg.seq + cfg.ctx_len, 2 * LANE), BF16),
                        pltpu.VMEM((nblk, qb, kwin + cfg.ctx_len), F32)],
    )
    return pl.pallas_call(
        functools.partial(_attn_b_kernel, nblk=nblk, qb=qb, kwin=kwin, w=w),
        grid_spec=grid_spec,
        out_shape=jax.ShapeDtypeStruct(o.shape, o.dtype),
        input_output_aliases={2: 0},
        compiler_params=_params("parallel", "arbitrary"),
        name="attn_neighbourhood",
    )(jnp.asarray(starts, jnp.int32), jnp.asarray(classes, jnp.int32), o, p, p, p, p, p, bias)


def _attn_ctx_kernel(o_in_ref, p_ref, o_ref, *, cfg):
    del o_in_ref
    group = cfg.heads_a // cfg.kv_a

    def cols(off, h):
        return p_ref[:, off + h * LANE: off + (h + 1) * LANE]

    for h in range(cfg.heads_a):
        g = h // group
        o = _softmax_pv([_dot_nt(cols(0, h), cols(cfg.off_ka, g))], [cols(cfg.off_va, g)])
        o_ref[:, h * LANE:(h + 1) * LANE] = o.astype(o_ref.dtype)
    for h in range(cfg.heads_b):
        o = _softmax_pv([_dot_nt(cols(cfg.wa_q, h), cols(cfg.off_kb, h))], [cols(cfg.off_vb, h)])
        o_ref[:, cfg.wa_q + h * LANE: cfg.wa_q + (h + 1) * LANE] = o.astype(o_ref.dtype)


def _attn_ctx(cfg, p, o):
    ctx_blk0 = cfg.n_lat // cfg.ctx_len
    return pl.pallas_call(
        functools.partial(_attn_ctx_kernel, cfg=cfg),
        grid=(cfg.batch,),
        in_specs=[
            pl.BlockSpec(memory_space=pl.ANY),
            pl.BlockSpec((cfg.ctx_len, cfg.in_cols), lambda b: (ctx_blk0 + b, 0)),
        ],
        out_specs=pl.BlockSpec((cfg.ctx_len, cfg.mix_w), lambda b: (ctx_blk0 + b, 0)),
        out_shape=jax.ShapeDtypeStruct(o.shape, o.dtype),
        input_output_aliases={0: 0},
        compiler_params=_params("parallel"),
        name="attn_context",
    )(o, p)


def _out_proj_kernel(o_ref, w_ref, x_ref, gate_ref, y_ref):
    half = y_ref.shape[1] // 2
    gate = gate_ref[0, 0]
    for c in (slice(0, half), slice(half, 2 * half)):
        y_ref[:, c] = x_ref[:, c] + gate[:, c] * _dot(o_ref[...], w_ref[:, c])


def _out_proj(cfg, tl, o, w_out, layer, x_all, mod, n_rows):
    tm, tn = tl.tm_proj, tl.tn_out
    n_tiles = n_rows // tm
    return pl.pallas_call(
        _out_proj_kernel,
        grid=(n_tiles, cfg.d_model // tn),
        in_specs=[
            pl.BlockSpec((tm, cfg.mix_w), lambda i, j: (i, 0)),
            pl.BlockSpec((None, cfg.mix_w, tn), lambda i, j: (layer, 0, j)),
            pl.BlockSpec((tm, tn), lambda i, j: (i, j)),
            _mod_spec(cfg, tm, 2, tn, 2, col_axis=1),
        ],
        out_specs=pl.BlockSpec((tm, tn), lambda i, j: (i, j)),
        out_shape=jax.ShapeDtypeStruct((n_rows, cfg.d_model), F32),
        compiler_params=_params("parallel", "arbitrary"),
        name="out_proj",
    )(o, w_out, x_all, mod)


def _router_kernel(x_ref, g_ref, shift_ref, scale_ref, wr_ref, h_ref, route_ref, *, n_experts):
    h = _modulated(x_ref[...], g_ref, shift_ref, scale_ref)
    h_ref[...] = h
    logits = jnp.dot(h, wr_ref[...], preferred_element_type=F32, precision=lax.Precision.HIGHEST)
    lane = lax.broadcasted_iota(jnp.int32, logits.shape, 1).astype(F32)
    neg = jnp.float32(-jnp.inf)
    logits = jnp.where(lane < n_experts, logits, neg)
    v1 = jnp.max(logits, axis=-1, keepdims=True)
    i1 = jnp.min(jnp.where(logits == v1, lane, float(LANE)), axis=-1, keepdims=True)
    rest = jnp.where(lane == i1, neg, logits)
    v2 = jnp.max(rest, axis=-1, keepdims=True)
    i2 = jnp.min(jnp.where(rest == v2, lane, float(LANE)), axis=-1, keepdims=True)
    w1 = 1.0 / (1.0 + jnp.exp(v2 - v1))
    route = jnp.where(lane == 0.0, w1, jnp.where(lane == 1.0, 1.0 - w1, jnp.where(lane == 2.0, i1, i2)))
    route_ref[...] = jnp.where(lane < 4.0, route, 0.0)


def _router(cfg, tl, x_all, g, mod, w_router_pad, n_tiles):
    tm, d = tl.tm, cfg.d_model
    return pl.pallas_call(
        functools.partial(_router_kernel, n_experts=cfg.n_experts),
        grid=(n_tiles,),
        in_specs=[
            pl.BlockSpec((tm, d), lambda i: (i, 0)),
            pl.BlockSpec((1, d), lambda i: (0, 0)),
            _mod_spec(cfg, tm, 3, d, 1),
            _mod_spec(cfg, tm, 4, d, 1),
            pl.BlockSpec((d, LANE), lambda i: (0, 0)),
        ],
        out_specs=[pl.BlockSpec((tm, d), lambda i: (i, 0)), pl.BlockSpec((tm, LANE), lambda i: (i, 0))],
        out_shape=[jax.ShapeDtypeStruct((n_tiles * tm, d), F32), jax.ShapeDtypeStruct((n_tiles * tm, LANE), F32)],
        compiler_params=_params("parallel"),
        name="router",
    )(x_all, g, mod, mod, w_router_pad)


def _swiglu_step(h, w1_ref, w3_ref, w2_ref):
    a = _dot(h, w1_ref[0])
    b = _dot(h, w3_ref[0])
    return _dot((a * jax.nn.sigmoid(a) * b).astype(BF16), w2_ref[0])


def _ffn_kernel(x_ref, g_ref, shift_ref, scale_ref, gate_ref, w1_ref, w3_ref, w2_ref, y_ref, h_ref):
    f = pl.program_id(1)

    @pl.when(f == 0)
    def _():
        h_ref[...] = _modulated(x_ref[...], g_ref, shift_ref, scale_ref).astype(BF16)
        y_ref[...] = jnp.zeros_like(y_ref)

    y_ref[...] += _swiglu_step(h_ref[...], w1_ref, w3_ref, w2_ref)

    @pl.when(f == pl.num_programs(1) - 1)
    def _():
        y_ref[...] = x_ref[...] + gate_ref[0, 0] * y_ref[...]


def _ffn(cfg, tl, x_all, g, mod, w1, w3, w2, layer, n_tiles):
    tm, tf, d = tl.tm, tl.tf, cfg.d_model
    return pl.pallas_call(
        _ffn_kernel,
        grid=(n_tiles, cfg.d_ff // tf),
        in_specs=[
            pl.BlockSpec((tm, d), lambda i, f: (i, 0)),
            pl.BlockSpec((1, d), lambda i, f: (0, 0)),
            _mod_spec(cfg, tm, 3, d, 2),
            _mod_spec(cfg, tm, 4, d, 2),
            _mod_spec(cfg, tm, 5, d, 2),
            pl.BlockSpec((1, d, tf), lambda i, f: (layer, 0, f)),
            pl.BlockSpec((1, d, tf), lambda i, f: (layer, 0, f)),
            pl.BlockSpec((1, tf, d), lambda i, f: (layer, f, 0)),
        ],
        out_specs=pl.BlockSpec((tm, d), lambda i, f: (i, 0)),
        out_shape=jax.ShapeDtypeStruct((n_tiles * tm, d), F32),
        scratch_shapes=[pltpu.VMEM((tm, d), BF16)],
        compiler_params=_params("parallel", "arbitrary"),
        name="ffn_dense",
    )(x_all, g, mod, mod, mod, w1, w3, w2)


def _route_plan(cfg, tm, tm_pad, route, n_rows):
    n_e = cfg.n_experts
    n_assign = 2 * n_rows
    n_tiles_max = n_assign // tm + n_e + 1
    expert = route[:, 2:4].astype(jnp.int32).reshape(n_assign)
    onehot = (expert[:, None] == jnp.arange(n_e)[None, :]).astype(jnp.int32)
    csum = jnp.cumsum(onehot, axis=0)
    rank = jnp.sum(csum * onehot, axis=1) - 1
    counts = csum[-1]
    tiles_per_e = (counts + tm - 1) // tm
    tile_end = jnp.cumsum(tiles_per_e)
    tile_start = tile_end - tiles_per_e
    n_used = tile_end[-1]
    pos = jnp.sum(onehot * tile_start[None, :], axis=1) * tm + rank
    tile_ids = jnp.arange(n_tiles_max)
    tile_expert = jnp.minimum(jnp.sum(tile_ids[:, None] >= tile_end[None, :], axis=1), n_e - 1)
    tile_expert = jnp.where(tile_ids < n_used, tile_expert, tile_expert[jnp.maximum(n_used - 1, 0)])
    order = jnp.argsort(expert, stable=True)
    unpadded_start = jnp.cumsum(counts) - counts
    k = (tile_ids - tile_start[tile_expert])[:, None] * tm + jnp.arange(tm)[None, :]
    valid = jnp.logical_and(k < counts[tile_expert][:, None], (tile_ids < n_used)[:, None])
    src = order[jnp.clip(unpadded_start[tile_expert][:, None] + k, 0, n_assign - 1)] // 2
    src_tok = jnp.where(valid, src, 0).astype(jnp.int32)
    src_tok = jnp.pad(src_tok, ((0, 0), (0, tm_pad - tm))).reshape(n_tiles_max, 1, tm_pad)
    pos = pos.astype(jnp.int32).reshape(n_rows // tm, 1, tm, 2)
    return (tile_expert.astype(jnp.int32), n_used.astype(jnp.int32).reshape(1), src_tok,
            pos[..., 0], pos[..., 1])


def _start_row_gather(idx_ref, src_hbm, dst_ref, sem, n_rows):
    def body(r, carry):
        pltpu.make_async_copy(src_hbm.at[pl.ds(idx_ref[0, 0, r], 1), :], dst_ref.at[pl.ds(r, 1), :], sem).start()
        return carry
    lax.fori_loop(0, n_rows, body, 0, unroll=8)


def _wait_row_gather(src_hbm, dst_ref, sem, n_rows):
    def body(r, carry):
        pltpu.make_async_copy(src_hbm.at[pl.ds(0, 1), :], dst_ref.at[pl.ds(r, 1), :], sem).wait()
        return carry
    lax.fori_loop(0, n_rows, body, 0, unroll=8)


def _moe_ffn_kernel(te_ref, nu_ref, src0_ref, src1_ref, h_hbm, w1_ref, w3_ref, w2_ref, y_ref,
                    hbuf, h_ref, sem, *, tm, rows_per_step):
    del te_ref
    t, f = pl.program_id(0), pl.program_id(1)
    n_used = nu_ref[0]
    slot = lax.rem(t, 2)
    live = t < n_used
    at_tile_start = f == 0
    tm_pad = hbuf.shape[1]

    @pl.when(jnp.logical_and(at_tile_start, t == 0))
    def _():
        _start_row_gather(src0_ref, h_hbm, hbuf.at[0], sem.at[0], tm_pad)

    @pl.when(jnp.logical_and(at_tile_start, t <= n_used))
    def _():
        _wait_row_gather(h_hbm, hbuf.at[slot], sem.at[slot], tm_pad)

    @pl.when(jnp.logical_and(at_tile_start, live))
    def _():
        h_ref[...] = hbuf[slot, :tm].astype(BF16)
        y_ref[...] = jnp.zeros_like(y_ref)

    @pl.when(live)
    def _():
        for k in range(rows_per_step):
            r = f * rows_per_step + k
            pltpu.make_async_copy(h_hbm.at[pl.ds(src1_ref[0, 0, r], 1), :],
                                  hbuf.at[1 - slot, pl.ds(r, 1), :], sem.at[1 - slot]).start()
        y_ref[...] += _swiglu_step(h_ref[...], w1_ref, w3_ref, w2_ref)


def _moe_ffn(cfg, tl, h, w1, w3, w2, layer, tile_expert, n_used, src_tok):
    tm, tf, d = tl.tm, tl.tf, cfg.d_model
    n_tiles_max, _, tm_pad = src_tok.shape
    nf = cfg.d_ff // tf
    assert tm_pad % nf == 0 and tm_pad >= tm

    def live_tile(t, nu):
        return jnp.minimum(t, nu[0] - 1)

    def f_idx(t, f, nu):
        return jnp.where(t < nu[0], f, nf - 1)

    grid_spec = pltpu.PrefetchScalarGridSpec(
        num_scalar_prefetch=2,
        grid=(n_tiles_max, nf),
        in_specs=[
            pl.BlockSpec((1, 1, tm_pad), lambda t, f, te, nu: (0, 0, 0), memory_space=pltpu.SMEM),
            pl.BlockSpec((1, 1, tm_pad), lambda t, f, te, nu: (live_tile(t + 1, nu), 0, 0),
                         memory_space=pltpu.SMEM),
            pl.BlockSpec(memory_space=pl.ANY),
            pl.BlockSpec((None, 1, d, tf), lambda t, f, te, nu: (layer, te[t], 0, f_idx(t, f, nu))),
            pl.BlockSpec((None, 1, d, tf), lambda t, f, te, nu: (layer, te[t], 0, f_idx(t, f, nu))),
            pl.BlockSpec((None, 1, tf, d), lambda t, f, te, nu: (layer, te[t], f_idx(t, f, nu), 0)),
        ],
        out_specs=pl.BlockSpec((tm, d), lambda t, f, te, nu: (live_tile(t, nu), 0)),
        scratch_shapes=[pltpu.VMEM((2, tm_pad, d), F32), pltpu.VMEM((tm, d), BF16),
                        pltpu.SemaphoreType.DMA((2,))],
    )
    return pl.pallas_call(
        functools.partial(_moe_ffn_kernel, tm=tm, rows_per_step=tm_pad // nf),
        grid_spec=grid_spec,
        out_shape=jax.ShapeDtypeStruct((n_tiles_max * tm, d), F32),
        compiler_params=_params("arbitrary", "arbitrary"),
        name="moe_ffn",
    )(tile_expert, n_used, src_tok, src_tok, h, w1, w3, w2)


def _moe_combine_kernel(pa0_ref, pb0_ref, pa1_ref, pb1_ref, y_hbm, x_ref, gate_ref, route_ref, o_ref,
                        ya, yb, sem, *, tm):
    i = pl.program_id(0)
    slot = lax.rem(i, 2)

    def start(pa_ref, pb_ref, s):
        _start_row_gather(pa_ref, y_hbm, ya.at[s], sem.at[s], tm)
        _start_row_gather(pb_ref, y_hbm, yb.at[s], sem.at[s], tm)

    @pl.when(i == 0)
    def _():
        start(pa0_ref, pb0_ref, 0)

    @pl.when(i + 1 < pl.num_programs(0))
    def _():
        start(pa1_ref, pb1_ref, 1 - slot)

    _wait_row_gather(y_hbm, ya.at[slot], sem.at[slot], tm)
    _wait_row_gather(y_hbm, yb.at[slot], sem.at[slot], tm)
    route = route_ref[...]
    y = route[:, 0:1] * ya[slot] + route[:, 1:2] * yb[slot]
    o_ref[...] = x_ref[...] + gate_ref[0, 0] * y


def _moe_combine(cfg, tl, y_sorted, x_all, mod, route, pos_a, pos_b, n_tiles):
    tm, d = tl.tm, cfg.d_model
    first = pl.BlockSpec((1, 1, tm), lambda i: (0, 0, 0), memory_space=pltpu.SMEM)
    nxt = pl.BlockSpec((1, 1, tm), lambda i: (jnp.minimum(i + 1, n_tiles - 1), 0, 0), memory_space=pltpu.SMEM)
    return pl.pallas_call(
        functools.partial(_moe_combine_kernel, tm=tm),
        grid=(n_tiles,),
        in_specs=[
            first, first, nxt, nxt,
            pl.BlockSpec(memory_space=pl.ANY),
            pl.BlockSpec((tm, d), lambda i: (i, 0)),
            _mod_spec(cfg, tm, 5, d, 1),
            pl.BlockSpec((tm, LANE), lambda i: (i, 0)),
        ],
        out_specs=pl.BlockSpec((tm, d), lambda i: (i, 0)),
        out_shape=jax.ShapeDtypeStruct((n_tiles * tm, d), F32),
        scratch_shapes=[pltpu.VMEM((2, tm, d), F32), pltpu.VMEM((2, tm, d), F32), pltpu.SemaphoreType.DMA((2,))],
        compiler_params=_params("arbitrary"),
        name="moe_combine",
    )(pos_a, pos_b, pos_a, pos_b, y_sorted, x_all, mod, route)


def _rope_tables(cfg, tm):
    half = cfg.head_dim // 2
    quarter = half // 2
    t = jnp.arange(cfg.seq)
    freqs = ROPE_THETA ** (-jnp.arange(quarter, dtype=F32) / quarter)
    ang_r = (t // cfg.grid_w).astype(F32)[:, None] * freqs[None, :]
    ang_c = (t % cfg.grid_w).astype(F32)[:, None] * freqs[None, :]
    cos = jnp.concatenate([jnp.cos(ang_r), jnp.cos(ang_c)] * 2, axis=-1)
    sin = jnp.concatenate([-jnp.sin(ang_r), -jnp.sin(ang_c), jnp.sin(ang_r), jnp.sin(ang_c)], axis=-1)
    cos = jnp.concatenate([cos, jnp.ones((tm, cfg.head_dim), F32)], axis=0)
    sin = jnp.concatenate([sin, jnp.zeros((tm, cfg.head_dim), F32)], axis=0)
    return cos, sin


def _forward(cfg, x, c, ctx, c_ctx, ada_w, ada_b, norm_attn, norm_ffn, w_in, qn_a, kn_a, qn_b, kn_b, rpb,
             w_out, w1_dense, w3_dense, w2_dense, w_router, w1_moe, w3_moe, w2_moe):
    tl = _pick_tiles(cfg)
    d = cfg.d_model
    n_tiles = cfg.n_tok // tl.tm
    n_lat_tiles = cfg.n_lat // tl.tm

    x_all = jnp.concatenate([x.reshape(cfg.n_lat, d), ctx.reshape(cfg.batch * cfg.ctx_len, d)], axis=0)

    cond = jnp.concatenate([c, c_ctx[None, :]], axis=0)
    pad_rows = -(-cond.shape[0] // 16) * 16
    cond_pad = jnp.zeros((pad_rows, d), F32).at[:cond.shape[0]].set(cond)
    mod_all = _ada_mod(cfg, tl, cond_pad, ada_w, ada_b)

    rope_cos, rope_sin = _rope_tables(cfg, tl.tm_proj)
    win, starts, classes, row_tables, col_table = _na_geometry(cfg, tl.rb)
    na_bias = _na_bias(rpb, row_tables, col_table)
    q_scale = cfg.head_dim ** -0.5 * LOG2E
    ones_a = jnp.ones((cfg.wa_kv,), F32)
    ones_b = jnp.ones((cfg.wb,), F32)
    w_in, w_out = w_in.astype(BF16), w_out.astype(BF16)
    def rotary_cols(w):
        return _rotary_layout(w.reshape(w.shape[:2] + (-1, cfg.head_dim)), 3).reshape(w.shape)
    w_in = jnp.concatenate([rotary_cols(w_in[:, :, :cfg.wa_q]), w_in[:, :, cfg.wa_q:cfg.off_ka],
                            rotary_cols(w_in[:, :, cfg.off_ka:cfg.off_va]), w_in[:, :, cfg.off_va:]], axis=2)
    qn_a, kn_a = _rotary_layout(qn_a, 1), _rotary_layout(kn_a, 1)
    w1_dense, w3_dense, w2_dense = (w.astype(BF16) for w in (w1_dense, w3_dense, w2_dense))
    w1_moe, w3_moe, w2_moe = (w.astype(BF16) for w in (w1_moe, w3_moe, w2_moe))
    nf = cfg.d_ff // tl.tf
    tm_pad = -(-tl.tm // (8 * nf)) * 8 * nf

    for i in range(cfg.depth):
        last = i == cfg.depth - 1
        j = i // 2
        mod = mod_all[i, :cfg.batch + 1].reshape(cfg.batch + 1, N_MOD, 1, d)
        gain = jnp.concatenate([
            jnp.tile(qn_a[i], cfg.heads_a) * q_scale, jnp.tile(qn_b[i], cfg.heads_b) * q_scale,
            jnp.tile(kn_a[i], cfg.kv_a), ones_a, jnp.tile(kn_b[i], cfg.heads_b), ones_b])[None, :]

        p = _in_proj(cfg, tl, x_all, norm_attn[i][None, :], mod, w_in, i, gain, rope_cos, rope_sin)
        o = _attn_a(cfg, tl, p)
        o = _attn_b(cfg, tl, p, o, na_bias, i, starts, classes, win)
        if not last:
            o = _attn_ctx(cfg, p, o)
        live_tiles = n_lat_tiles if last else n_tiles
        x_all = _out_proj(cfg, tl, o, w_out, i, x_all, mod, live_tiles * tl.tm)

        g_ffn = norm_ffn[i][None, :]
        if i % 2 == 0:
            x_all = _ffn(cfg, tl, x_all, g_ffn, mod, w1_dense, w3_dense, w2_dense, j, live_tiles)
        else:
            wr = jnp.zeros((d, LANE), F32).at[:, :cfg.n_experts].set(w_router[j])
            h, route = _router(cfg, tl, x_all, g_ffn, mod, wr, live_tiles)
            tile_expert, n_used, src_tok, pos_a, pos_b = _route_plan(cfg, tl.tm, tm_pad, route,
                                                                     live_tiles * tl.tm)
            y_sorted = _moe_ffn(cfg, tl, h, w1_moe, w3_moe, w2_moe, j, tile_expert, n_used, src_tok)
            x_all = _moe_combine(cfg, tl, y_sorted, x_all, mod, route, pos_a, pos_b, live_tiles)

    return x_all.reshape(cfg.batch, cfg.seq, d)


def kernel(x, c, ctx, c_ctx, ada_w, ada_b, norm_attn, norm_ffn, w_in, qn_a, kn_a, qn_b, kn_b, rpb, w_out,
           w1_dense, w3_dense, w2_dense, w_router, w1_moe, w3_moe, w2_moe):
    batch, seq, d_model = x.shape
    depth = w_in.shape[0]
    head_dim = qn_a.shape[-1]
    heads_b = rpb.shape[1]
    wb = heads_b * head_dim
    wa_q = w_out.shape[1] - wb
    wa_kv = (w_in.shape[2] - wa_q - 3 * wb) // 2
    cfg = Cfg(d_model=d_model, batch=batch, seq=seq, ctx_len=ctx.shape[1], grid_w=64, head_dim=head_dim,
              heads_a=wa_q // head_dim, kv_a=wa_kv // head_dim, heads_b=heads_b,
              na_kh=(rpb.shape[2] + 1) // 2, na_kw=(rpb.shape[3] + 1) // 2, d_ff=w1_dense.shape[-1],
              n_experts=w_router.shape[-1], depth=depth)
    return _forward(cfg, x, c, ctx, c_ctx, ada_w, ada_b, norm_attn, norm_ffn, w_in, qn_a, kn_a, qn_b, kn_b,
                    rpb, w_out, w1_dense, w3_dense, w2_dense, w_router, w1_moe, w3_moe, w2_moe)
```

```python
import functools
from typing import NamedTuple

import numpy as np
import jax
import jax.numpy as jnp
from jax import lax
from jax.experimental import pallas as pl
from jax.experimental.pallas import tpu as pltpu

F32 = jnp.float32
BF16 = jnp.bfloat16

EPS = 1e-6
ROPE_THETA = 10000.0
N_MOD = 6
LANE = 128
MASK_VALUE = -1e30
LOG2E = 1.4426950408889634
V7X_VMEM_LIMIT = 56 * 1024 * 1024


class Cfg(NamedTuple):
    d_model: int
    batch: int
    seq: int
    ctx_len: int
    grid_w: int
    head_dim: int
    heads_a: int
    kv_a: int
    heads_b: int
    na_kh: int
    na_kw: int
    d_ff: int
    n_experts: int
    depth: int

    @property
    def wa_q(self):
        return self.heads_a * self.head_dim

    @property
    def wa_kv(self):
        return self.kv_a * self.head_dim

    @property
    def wb(self):
        return self.heads_b * self.head_dim

    @property
    def mix_w(self):
        return self.wa_q + self.wb

    @property
    def in_cols(self):
        return self.mix_w + 2 * self.wa_kv + 2 * self.wb

    @property
    def rows(self):
        return self.seq // self.grid_w

    @property
    def n_lat(self):
        return self.batch * self.seq

    @property
    def n_tok(self):
        return self.batch * (self.seq + self.ctx_len)

    @property
    def off_ka(self):
        return self.mix_w

    @property
    def off_va(self):
        return self.mix_w + self.wa_kv

    @property
    def off_kb(self):
        return self.mix_w + 2 * self.wa_kv

    @property
    def off_vb(self):
        return self.mix_w + 2 * self.wa_kv + self.wb


class Tiles(NamedTuple):
    tm: int
    tm_proj: int
    hps: int
    tn_out: int
    tf: int
    tq: int
    rb: int
    tn_ada: int


def _pick_tiles(cfg):
    n_ctx = cfg.batch * cfg.ctx_len
    tm = min(512, cfg.seq, n_ctx)
    tm_proj = min(1024, cfg.seq, n_ctx)
    for t in (tm, tm_proj):
        assert cfg.seq % t == 0 and n_ctx % t == 0
    n_heads = cfg.in_cols // LANE
    hps = next(h for h in (12, 10, 8, 6, 4, 2) if n_heads % h == 0)
    tn_out = min(1024, cfg.d_model)
    tf = min(512, cfg.d_ff)
    assert cfg.d_model % tn_out == 0 and cfg.d_ff % tf == 0
    tq = min(256, cfg.seq)
    assert cfg.seq % tq == 0
    rb = 4
    assert cfg.rows % (2 * rb) == 0 and cfg.rows >= rb + cfg.na_kh
    tn_ada = next(t for t in (1024, 512, 256, LANE) if (N_MOD * cfg.d_model) % t == 0)
    return Tiles(tm, tm_proj, hps, tn_out, tf, tq, rb, tn_ada)


def _params(*sem):
    return pltpu.CompilerParams(dimension_semantics=sem, vmem_limit_bytes=V7X_VMEM_LIMIT)


def _dot(a, b):
    return jnp.dot(a, b, preferred_element_type=F32)


def _dot_nt(a, b):
    return lax.dot_general(a, b, (((1,), (1,)), ((), ())), preferred_element_type=F32)


def _rms(x):
    return x * lax.rsqrt(jnp.mean(x * x, axis=-1, keepdims=True) + EPS)


def _ada_kernel(cond_ref, w_ref, b_ref, o_ref):
    c = cond_ref[...]
    c = (c * jax.nn.sigmoid(c)).astype(BF16)
    o_ref[0] = _dot(c, w_ref[0].astype(BF16)) + b_ref[0]


def _ada_mod(cfg, tl, cond_pad, ada_w, ada_b):
    depth, d, n6 = ada_w.shape
    mp = cond_pad.shape[0]
    return pl.pallas_call(
        _ada_kernel,
        grid=(depth, n6 // tl.tn_ada),
        in_specs=[
            pl.BlockSpec((mp, d), lambda l, j: (0, 0)),
            pl.BlockSpec((1, d, tl.tn_ada), lambda l, j: (l, 0, j)),
            pl.BlockSpec((1, 1, tl.tn_ada), lambda l, j: (l, 0, j)),
        ],
        out_specs=pl.BlockSpec((1, mp, tl.tn_ada), lambda l, j: (l, 0, j)),
        out_shape=jax.ShapeDtypeStruct((depth, mp, n6), F32),
        compiler_params=_params("parallel", "parallel"),
        name="ada_mod",
    )(cond_pad, ada_w, ada_b.reshape(depth, 1, n6))


def _mod_index(cfg, tm):
    n_lat_tiles = cfg.n_lat // tm
    per_sample = cfg.seq // tm

    def f(i):
        return jnp.where(i < n_lat_tiles, i // per_sample, cfg.batch)
    return f


def _mod_spec(cfg, tm, k, ncols, grid_rank, col_axis=None):
    mi = _mod_index(cfg, tm)
    if col_axis is None:
        return pl.BlockSpec((1, 1, 1, ncols), lambda i, *rest: (mi(i), k, 0, 0))
    assert grid_rank == 2 and col_axis == 1
    return pl.BlockSpec((1, 1, 1, ncols), lambda i, j: (mi(i), k, 0, j))


def _modulated(x, g_ref, shift_ref, scale_ref):
    y = _rms(x) * g_ref[...]
    return y * (1.0 + scale_ref[0, 0]) + shift_ref[0, 0]


def _rotary_partner(y):
    return pltpu.roll(y, LANE // 2, 1)


def _rotary_layout(a, axis):
    shape = a.shape
    quarter = shape[axis] // 4
    a = a.reshape(shape[:axis] + (2, 2, quarter) + shape[axis + 1:])
    return jnp.swapaxes(a, axis, axis + 1).reshape(shape)


HEAD_ROPE, HEAD_NORM, HEAD_PLAIN = "rope", "norm", "plain"


def _proj_kernel(x_ref, g_ref, shift_ref, scale_ref, w_ref, gain_ref, cos_ref, sin_ref, o_ref, h_ref,
                 *, step_kinds):
    j = pl.program_id(1)

    @pl.when(j == 0)
    def _():
        h_ref[...] = _modulated(x_ref[...], g_ref, shift_ref, scale_ref).astype(BF16)

    def head_epilogue(y, kind, k):
        if kind != HEAD_PLAIN:
            y = _rms(y) * gain_ref[:, k * LANE:(k + 1) * LANE]
        if kind == HEAD_ROPE:
            y = y * cos_ref[...] + _rotary_partner(y) * sin_ref[...]
        return y

    def run(kinds):
        for c in range(0, len(kinds), 2):
            acc = _dot(h_ref[...], w_ref[:, c * LANE:(c + 2) * LANE])
            for k in (c, c + 1):
                y = head_epilogue(acc[:, (k - c) * LANE:(k - c + 1) * LANE], kinds[k], k)
                o_ref[:, k * LANE:(k + 1) * LANE] = y.astype(o_ref.dtype)

    for kinds in sorted(set(step_kinds)):
        steps = [s for s, kk in enumerate(step_kinds) if kk == kinds]
        cond = functools.reduce(jnp.logical_or, [j == s for s in steps])
        pl.when(cond)(functools.partial(run, kinds))


def _in_proj(cfg, tl, x_all, g, mod, w_in, layer, gain, rope_cos, rope_sin):
    tm, tn = tl.tm_proj, tl.hps * LANE
    d = cfg.d_model
    n_tiles = cfg.n_tok // tm
    n_lat_tiles = cfg.n_lat // tm
    per_sample = cfg.seq // tm
    head_kinds = ([HEAD_ROPE] * cfg.heads_a + [HEAD_NORM] * cfg.heads_b + [HEAD_ROPE] * cfg.kv_a
                  + [HEAD_PLAIN] * cfg.kv_a + [HEAD_NORM] * cfg.heads_b + [HEAD_PLAIN] * cfg.heads_b)
    step_kinds = tuple(tuple(head_kinds[s:s + tl.hps]) for s in range(0, len(head_kinds), tl.hps))
    kern = functools.partial(_proj_kernel, step_kinds=step_kinds)

    def rope_idx(i, j):
        return (jnp.where(i < n_lat_tiles, i % per_sample, per_sample), 0)

    return pl.pallas_call(
        kern,
        grid=(n_tiles, cfg.in_cols // tn),
        in_specs=[
            pl.BlockSpec((tm, d), lambda i, j: (i, 0)),
            pl.BlockSpec((1, d), lambda i, j: (0, 0)),
            _mod_spec(cfg, tm, 0, d, 2),
            _mod_spec(cfg, tm, 1, d, 2),
            pl.BlockSpec((None, d, tn), lambda i, j: (layer, 0, j)),
            pl.BlockSpec((1, tn), lambda i, j: (0, j)),
            pl.BlockSpec((tm, LANE), rope_idx),
            pl.BlockSpec((tm, LANE), rope_idx),
        ],
        out_specs=pl.BlockSpec((tm, tn), lambda i, j: (i, j)),
        out_shape=jax.ShapeDtypeStruct((cfg.n_tok, cfg.in_cols), BF16),
        scratch_shapes=[pltpu.VMEM((tm, d), BF16)],
        compiler_params=_params("parallel", "arbitrary"),
        name="in_proj",
    )(x_all, g, mod, mod, w_in, gain, rope_cos, rope_sin)


def _softmax_pv(s_parts, v_parts):
    m = None
    for s in s_parts:
        mx = jnp.max(s, axis=-1, keepdims=True)
        m = mx if m is None else jnp.maximum(m, mx)
    l = None
    o = None
    for s, v in zip(s_parts, v_parts):
        p = jnp.exp2(s - m)
        ps = jnp.sum(p, axis=-1, keepdims=True)
        pv = _dot(p.astype(BF16), v)
        l = ps if l is None else l + ps
        o = pv if o is None else o + pv
    return o / l


def _softmax_pv_ext(s_parts, vext_parts):
    m = None
    for s in s_parts:
        mx = jnp.max(s, axis=-1, keepdims=True)
        m = mx if m is None else jnp.maximum(m, mx)
    oe = None
    for s, v in zip(s_parts, vext_parts):
        pv = _dot(jnp.exp2(s - m).astype(BF16), v)
        oe = pv if oe is None else oe + pv
    return oe[:, :LANE] / oe[:, LANE:]


def _softmax_pv_given_max(s, m, v_ext):
    oe = _dot(jnp.exp2(s - m).astype(BF16), v_ext)
    return oe[:, :LANE] / oe[:, LANE:]


def _fill_kv_ext(k_all, v_ext, kl_ref, vl_ref, kc_ref, vc_ref):
    n_lat = kl_ref.shape[0]
    k_all[:n_lat, :] = kl_ref[...]
    k_all[n_lat:, :] = kc_ref[...]
    v_ext[:n_lat, :LANE] = vl_ref[...]
    v_ext[n_lat:, :LANE] = vc_ref[...]
    v_ext[:, LANE:] = jnp.ones((v_ext.shape[0], LANE), v_ext.dtype)


def _attn_a_kernel(q_ref, kl_ref, vl_ref, kc_ref, vc_ref, o_ref, k_all, v_ext, s_buf, m_buf, *, group):
    tq = q_ref.shape[0]
    phase = pl.program_id(3)

    @pl.when(jnp.logical_and(pl.program_id(2) == 0, phase == 0))
    def _():
        _fill_kv_ext(k_all, v_ext, kl_ref, vl_ref, kc_ref, vc_ref)

    @pl.when(phase == 0)
    def _():
        q = jnp.concatenate([q_ref[:, h * LANE:(h + 1) * LANE] for h in range(group)], axis=0)
        s = _dot_nt(q, k_all[...])
        s_buf[...] = s
        m_buf[...] = jnp.max(s, axis=-1, keepdims=True)

    @pl.when(phase == 1)
    def _():
        o = _softmax_pv_given_max(s_buf[...], m_buf[...], v_ext[...])
        for h in range(group):
            o_ref[:, h * LANE:(h + 1) * LANE] = o[h * tq:(h + 1) * tq].astype(o_ref.dtype)


def _attn_a(cfg, tl, p):
    assert cfg.head_dim == LANE
    group = cfg.heads_a // cfg.kv_a
    gw = group * LANE
    tq = tl.tq
    nq = cfg.seq // tq
    ctx_blk0 = cfg.n_lat // cfg.ctx_len
    ka0, va0 = cfg.off_ka // LANE, cfg.off_va // LANE
    n_keys = cfg.seq + cfg.ctx_len
    return pl.pallas_call(
        functools.partial(_attn_a_kernel, group=group),
        grid=(cfg.batch, cfg.kv_a, nq, 2),
        scratch_shapes=[pltpu.VMEM((n_keys, LANE), BF16), pltpu.VMEM((n_keys, 2 * LANE), BF16),
                        pltpu.VMEM((group * tq, n_keys), F32), pltpu.VMEM((group * tq, 1), F32)],
        in_specs=[
            pl.BlockSpec((tq, gw), lambda b, g, i, ph: (b * nq + i, g)),
            pl.BlockSpec((cfg.seq, LANE), lambda b, g, i, ph: (b, ka0 + g)),
            pl.BlockSpec((cfg.seq, LANE), lambda b, g, i, ph: (b, va0 + g)),
            pl.BlockSpec((cfg.ctx_len, LANE), lambda b, g, i, ph: (ctx_blk0 + b, ka0 + g)),
            pl.BlockSpec((cfg.ctx_len, LANE), lambda b, g, i, ph: (ctx_blk0 + b, va0 + g)),
        ],
        out_specs=pl.BlockSpec((tq, gw), lambda b, g, i, ph: (b * nq + i, g)),
        out_shape=jax.ShapeDtypeStruct((cfg.n_tok, cfg.mix_w), BF16),
        compiler_params=_params("arbitrary", "arbitrary", "arbitrary", "arbitrary"),
        name="attn_global",
    )(p, p, p, p, p)


def _na_geometry(cfg, rb):
    rows, w = cfg.rows, cfg.grid_w
    kh = min(cfg.na_kh, rows)
    kw = cfg.na_kw
    win = min(rows, -(-(rb - 1 + kh) // rb) * rb)
    nblk = rows // rb
    r0 = np.clip(np.arange(rows) - kh // 2, 0, rows - kh)
    c0 = np.clip(np.arange(w) - kw // 2, 0, w - kw)
    starts = np.clip(rb * np.arange(nblk) - kh // 2, 0, rows - win)
    col = np.arange(w)
    vc = (col[None, :] >= c0[:, None]) & (col[None, :] < c0[:, None] + kw)
    dc = np.where(vc, col[None, :] - col[:, None] + (cfg.na_kw - 1), 0)

    def row_table(i):
        r = rb * i + np.arange(rb)
        key_r = starts[i] + np.arange(win)
        vr = (key_r[None, :] >= r0[r][:, None]) & (key_r[None, :] < r0[r][:, None] + kh)
        assert (vr.sum(axis=1) == kh).all()
        return vr, np.where(vr, key_r[None, :] - r[:, None] + (cfg.na_kh - 1), 0)

    row_tables, classes = [], []
    for i in range(nblk):
        t = row_table(i)
        for c, u in enumerate(row_tables):
            if all(np.array_equal(a, b) for a, b in zip(t, u)):
                classes.append(c)
                break
        else:
            classes.append(len(row_tables))
            row_tables.append(t)
    return win, starts, np.asarray(classes), row_tables, (vc, dc)


def _na_bias(rpb, row_tables, col_table):
    depth, h, n_dr, n_dc = rpb.shape
    vc, dc = col_table
    w = vc.shape[0]
    onehot = (dc[None] == np.arange(n_dc)[:, None, None]).astype(np.float32)
    t = jnp.einsum("lhdk,kqc->lhdqc", rpb, onehot, precision=lax.Precision.HIGHEST)
    vr = np.stack([v for v, _ in row_tables])
    dr = np.stack([d for _, d in row_tables])
    ncls, rb, win = vr.shape
    b = jnp.take(t, dr.reshape(-1), axis=2).reshape(depth, h, ncls, rb, win, w, w)
    valid = vr[:, :, :, None, None] & vc[None, None, None]
    b = jnp.where(valid[None, None], b * LOG2E, MASK_VALUE).transpose(0, 1, 2, 3, 5, 4, 6)
    return b.reshape(depth, h, ncls, rb * w, win * w).astype(F32)


def _attn_b_kernel(start_ref, cls_ref, o_in_ref, q_ref, kl_ref, vl_ref, kc_ref, vc_ref, bias_ref, o_ref,
                   k_all, v_ext, s_buf, *, nblk, qb, kwin, w):
    del o_in_ref
    _fill_kv_ext(k_all, v_ext, kl_ref, vl_ref, kc_ref, vc_ref)
    n_lat = kl_ref.shape[0]

    def rows(i):
        return pl.ds(pl.multiple_of(i * qb, qb), qb)

    def window(i):
        return pl.ds(pl.multiple_of(start_ref[i] * w, w), kwin)

    def scores(i, carry):
        q = q_ref[rows(i), :]
        s_buf[i, :, :kwin] = _dot_nt(q, k_all[window(i), :]) + bias_ref[0, cls_ref[i]]
        s_buf[i, :, kwin:] = _dot_nt(q, k_all[n_lat:, :])
        return carry

    def outputs(i, carry):
        s = s_buf[i]
        p = jnp.exp2(s - jnp.max(s, axis=-1, keepdims=True)).astype(BF16)
        oe = _dot(p[:, :kwin], v_ext[window(i), :]) + _dot(p[:, kwin:], v_ext[n_lat:, :])
        o_ref[rows(i), :] = (oe[:, :LANE] / oe[:, LANE:]).astype(o_ref.dtype)
        return carry

    lax.fori_loop(0, nblk, scores, 0, unroll=4)
    lax.fori_loop(0, nblk, outputs, 0, unroll=4)


def _attn_b(cfg, tl, p, o, bias, layer, starts, classes, win):
    w = cfg.grid_w
    qb, kwin = tl.rb * w, win * w
    nblk = cfg.rows // tl.rb
    ncls = bias.shape[2]
    ctx_blk0 = cfg.n_lat // cfg.ctx_len
    q0, k0, v0 = cfg.wa_q // LANE, cfg.off_kb // LANE, cfg.off_vb // LANE
    grid_spec = pltpu.PrefetchScalarGridSpec(
        num_scalar_prefetch=2,
        grid=(cfg.heads_b, cfg.batch),
        in_specs=[
            pl.BlockSpec(memory_space=pl.ANY),
            pl.BlockSpec((cfg.seq, LANE), lambda h, b, s, c: (b, q0 + h)),
            pl.BlockSpec((cfg.seq, LANE), lambda h, b, s, c: (b, k0 + h)),
            pl.BlockSpec((cfg.seq, LANE), lambda h, b, s, c: (b, v0 + h)),
            pl.BlockSpec((cfg.ctx_len, LANE), lambda h, b, s, c: (ctx_blk0 + b, k0 + h)),
            pl.BlockSpec((cfg.ctx_len, LANE), lambda h, b, s, c: (ctx_blk0 + b, v0 + h)),
            pl.BlockSpec((None, 1, ncls, qb, kwin), lambda h, b, s, c: (layer, h, 0, 0, 0)),
        ],
        out_specs=pl.BlockSpec((cfg.seq, LANE), lambda h, b, s, c: (b, q0 + h)),
        scratch_shapes=[pltpu.VMEM((cfg.seq + cfg.ctx_len, LANE), BF16),
                        pltpu.VMEM((cfg.seq + cfg.ctx_len, 2 * LANE), BF16),
                        pltpu.VMEM((nblk, qb, kwin + cfg.ctx_len), F32)],
    )
    return pl.pallas_call(
        functools.partial(_attn_b_kernel, nblk=nblk, qb=qb, kwin=kwin, w=w),
        grid_spec=grid_spec,
        out_shape=jax.ShapeDtypeStruct(o.shape, o.dtype),
        input_output_aliases={2: 0},
        compiler_params=_params("parallel", "arbitrary"),
        name="attn_neighbourhood",
    )(jnp.asarray(starts, jnp.int32), jnp.asarray(classes, jnp.int32), o, p, p, p, p, p, bias)


def _attn_ctx_kernel(o_in_ref, p_ref, o_ref, *, cfg):
    del o_in_ref
    group = cfg.heads_a // cfg.kv_a

    def cols(off, h):
        return p_ref[:, off + h * LANE: off + (h + 1) * LANE]

    for h in range(cfg.heads_a):
        g = h // group
        o = _softmax_pv([_dot_nt(cols(0, h), cols(cfg.off_ka, g))], [cols(cfg.off_va, g)])
        o_ref[:, h * LANE:(h + 1) * LANE] = o.astype(o_ref.dtype)
    for h in range(cfg.heads_b):
        o = _softmax_pv([_dot_nt(cols(cfg.wa_q, h), cols(cfg.off_kb, h))], [cols(cfg.off_vb, h)])
        o_ref[:, cfg.wa_q + h * LANE: cfg.wa_q + (h + 1) * LANE] = o.astype(o_ref.dtype)


def _attn_ctx(cfg, p, o):
    ctx_blk0 = cfg.n_lat // cfg.ctx_len
    return pl.pallas_call(
        functools.partial(_attn_ctx_kernel, cfg=cfg),
        grid=(cfg.batch,),
        in_specs=[
            pl.BlockSpec(memory_space=pl.ANY),
            pl.BlockSpec((cfg.ctx_len, cfg.in_cols), lambda b: (ctx_blk0 + b, 0)),
        ],
        out_specs=pl.BlockSpec((cfg.ctx_len, cfg.mix_w), lambda b: (ctx_blk0 + b, 0)),
        out_shape=jax.ShapeDtypeStruct(o.shape, o.dtype),
        input_output_aliases={0: 0},
        compiler_params=_params("parallel"),
        name="attn_context",
    )(o, p)


def _out_proj_kernel(o_ref, w_ref, x_ref, gate_ref, y_ref):
    half = y_ref.shape[1] // 2
    gate = gate_ref[0, 0]
    for c in (slice(0, half), slice(half, 2 * half)):
        y_ref[:, c] = x_ref[:, c] + gate[:, c] * _dot(o_ref[...], w_ref[:, c])


def _out_proj(cfg, tl, o, w_out, layer, x_all, mod, n_rows):
    tm, tn = tl.tm_proj, tl.tn_out
    n_tiles = n_rows // tm
    return pl.pallas_call(
        _out_proj_kernel,
        grid=(n_tiles, cfg.d_model // tn),
        in_specs=[
            pl.BlockSpec((tm, cfg.mix_w), lambda i, j: (i, 0)),
            pl.BlockSpec((None, cfg.mix_w, tn), lambda i, j: (layer, 0, j)),
            pl.BlockSpec((tm, tn), lambda i, j: (i, j)),
            _mod_spec(cfg, tm, 2, tn, 2, col_axis=1),
        ],
        out_specs=pl.BlockSpec((tm, tn), lambda i, j: (i, j)),
        out_shape=jax.ShapeDtypeStruct((n_rows, cfg.d_model), F32),
        compiler_params=_params("parallel", "arbitrary"),
        name="out_proj",
    )(o, w_out, x_all, mod)


def _router_kernel(x_ref, g_ref, shift_ref, scale_ref, wr_ref, h_ref, route_ref, *, n_experts):
    h = _modulated(x_ref[...], g_ref, shift_ref, scale_ref)
    h_ref[...] = h
    logits = jnp.dot(h, wr_ref[...], preferred_element_type=F32, precision=lax.Precision.HIGHEST)
    lane = lax.broadcasted_iota(jnp.int32, logits.shape, 1).astype(F32)
    neg = jnp.float32(-jnp.inf)
    logits = jnp.where(lane < n_experts, logits, neg)
    v1 = jnp.max(logits, axis=-1, keepdims=True)
    i1 = jnp.min(jnp.where(logits == v1, lane, float(LANE)), axis=-1, keepdims=True)
    rest = jnp.where(lane == i1, neg, logits)
    v2 = jnp.max(rest, axis=-1, keepdims=True)
    i2 = jnp.min(jnp.where(rest == v2, lane, float(LANE)), axis=-1, keepdims=True)
    w1 = 1.0 / (1.0 + jnp.exp(v2 - v1))
    route = jnp.where(lane == 0.0, w1, jnp.where(lane == 1.0, 1.0 - w1, jnp.where(lane == 2.0, i1, i2)))
    route_ref[...] = jnp.where(lane < 4.0, route, 0.0)


def _router(cfg, tl, x_all, g, mod, w_router_pad, n_tiles):
    tm, d = tl.tm, cfg.d_model
    return pl.pallas_call(
        functools.partial(_router_kernel, n_experts=cfg.n_experts),
        grid=(n_tiles,),
        in_specs=[
            pl.BlockSpec((tm, d), lambda i: (i, 0)),
            pl.BlockSpec((1, d), lambda i: (0, 0)),
            _mod_spec(cfg, tm, 3, d, 1),
            _mod_spec(cfg, tm, 4, d, 1),
            pl.BlockSpec((d, LANE), lambda i: (0, 0)),
        ],
        out_specs=[pl.BlockSpec((tm, d), lambda i: (i, 0)), pl.BlockSpec((tm, LANE), lambda i: (i, 0))],
        out_shape=[jax.ShapeDtypeStruct((n_tiles * tm, d), F32), jax.ShapeDtypeStruct((n_tiles * tm, LANE), F32)],
        compiler_params=_params("parallel"),
        name="router",
    )(x_all, g, mod, mod, w_router_pad)


def _swiglu_step(h, w1_ref, w3_ref, w2_ref):
    a = _dot(h, w1_ref[0])
    b = _dot(h, w3_ref[0])
    return _dot((a * jax.nn.sigmoid(a) * b).astype(BF16), w2_ref[0])


def _ffn_kernel(x_ref, g_ref, shift_ref, scale_ref, gate_ref, w1_ref, w3_ref, w2_ref, y_ref, h_ref):
    f = pl.program_id(1)

    @pl.when(f == 0)
    def _():
        h_ref[...] = _modulated(x_ref[...], g_ref, shift_ref, scale_ref).astype(BF16)
        y_ref[...] = jnp.zeros_like(y_ref)

    y_ref[...] += _swiglu_step(h_ref[...], w1_ref, w3_ref, w2_ref)

    @pl.when(f == pl.num_programs(1) - 1)
    def _():
        y_ref[...] = x_ref[...] + gate_ref[0, 0] * y_ref[...]


def _ffn(cfg, tl, x_all, g, mod, w1, w3, w2, layer, n_tiles):
    tm, tf, d = tl.tm, tl.tf, cfg.d_model
    return pl.pallas_call(
        _ffn_kernel,
        grid=(n_tiles, cfg.d_ff // tf),
        in_specs=[
            pl.BlockSpec((tm, d), lambda i, f: (i, 0)),
            pl.BlockSpec((1, d), lambda i, f: (0, 0)),
            _mod_spec(cfg, tm, 3, d, 2),
            _mod_spec(cfg, tm, 4, d, 2),
            _mod_spec(cfg, tm, 5, d, 2),
            pl.BlockSpec((1, d, tf), lambda i, f: (layer, 0, f)),
            pl.BlockSpec((1, d, tf), lambda i, f: (layer, 0, f)),
            pl.BlockSpec((1, tf, d), lambda i, f: (layer, f, 0)),
        ],
        out_specs=pl.BlockSpec((tm, d), lambda i, f: (i, 0)),
        out_shape=jax.ShapeDtypeStruct((n_tiles * tm, d), F32),
        scratch_shapes=[pltpu.VMEM((tm, d), BF16)],
        compiler_params=_params("parallel", "arbitrary"),
        name="ffn_dense",
    )(x_all, g, mod, mod, mod, w1, w3, w2)


def _route_plan(cfg, tm, tm_pad, route, n_rows):
    n_e = cfg.n_experts
    n_assign = 2 * n_rows
    n_tiles_max = n_assign // tm + n_e + 1
    expert = route[:, 2:4].astype(jnp.int32).reshape(n_assign)
    onehot = (expert[:, None] == jnp.arange(n_e)[None, :]).astype(jnp.int32)
    csum = jnp.cumsum(onehot, axis=0)
    rank = jnp.sum(csum * onehot, axis=1) - 1
    counts = csum[-1]
    tiles_per_e = (counts + tm - 1) // tm
    tile_end = jnp.cumsum(tiles_per_e)
    tile_start = tile_end - tiles_per_e
    n_used = tile_end[-1]
    pos = jnp.sum(onehot * tile_start[None, :], axis=1) * tm + rank
    tile_ids = jnp.arange(n_tiles_max)
    tile_expert = jnp.minimum(jnp.sum(tile_ids[:, None] >= tile_end[None, :], axis=1), n_e - 1)
    tile_expert = jnp.where(tile_ids < n_used, tile_expert, tile_expert[jnp.maximum(n_used - 1, 0)])
    order = jnp.argsort(expert, stable=True)
    unpadded_start = jnp.cumsum(counts) - counts
    k = (tile_ids - tile_start[tile_expert])[:, None] * tm + jnp.arange(tm)[None, :]
    valid = jnp.logical_and(k < counts[tile_expert][:, None], (tile_ids < n_used)[:, None])
    src = order[jnp.clip(unpadded_start[tile_expert][:, None] + k, 0, n_assign - 1)] // 2
    src_tok = jnp.where(valid, src, 0).astype(jnp.int32)
    src_tok = jnp.pad(src_tok, ((0, 0), (0, tm_pad - tm))).reshape(n_tiles_max, 1, tm_pad)
    pos = pos.astype(jnp.int32).reshape(n_rows // tm, 1, tm, 2)
    return (tile_expert.astype(jnp.int32), n_used.astype(jnp.int32).reshape(1), src_tok,
            pos[..., 0], pos[..., 1])


def _start_row_gather(idx_ref, src_hbm, dst_ref, sem, n_rows):
    def body(r, carry):
        pltpu.make_async_copy(src_hbm.at[pl.ds(idx_ref[0, 0, r], 1), :], dst_ref.at[pl.ds(r, 1), :], sem).start()
        return carry
    lax.fori_loop(0, n_rows, body, 0, unroll=8)


def _wait_row_gather(src_hbm, dst_ref, sem, n_rows):
    assert dst_ref.shape[0] == n_rows
    pltpu.make_async_copy(src_hbm.at[pl.ds(0, n_rows), :], dst_ref, sem).wait()


def _moe_ffn_kernel(te_ref, nu_ref, src0_ref, src1_ref, h_hbm, w1_ref, w3_ref, w2_ref, y_ref,
                    hbuf, h_ref, sem, *, tm, rows_per_step):
    del te_ref
    t, f = pl.program_id(0), pl.program_id(1)
    n_used = nu_ref[0]
    slot = lax.rem(t, 2)
    live = t < n_used
    at_tile_start = f == 0
    tm_pad = hbuf.shape[1]

    @pl.when(jnp.logical_and(at_tile_start, t == 0))
    def _():
        _start_row_gather(src0_ref, h_hbm, hbuf.at[0], sem.at[0], tm_pad)

    @pl.when(jnp.logical_and(at_tile_start, t <= n_used))
    def _():
        _wait_row_gather(h_hbm, hbuf.at[slot], sem.at[slot], tm_pad)

    @pl.when(jnp.logical_and(at_tile_start, live))
    def _():
        h_ref[...] = hbuf[slot, :tm].astype(BF16)
        y_ref[...] = jnp.zeros_like(y_ref)

    @pl.when(live)
    def _():
        for k in range(rows_per_step):
            r = f * rows_per_step + k
            pltpu.make_async_copy(h_hbm.at[pl.ds(src1_ref[0, 0, r], 1), :],
                                  hbuf.at[1 - slot, pl.ds(r, 1), :], sem.at[1 - slot]).start()
        y_ref[...] += _swiglu_step(h_ref[...], w1_ref, w3_ref, w2_ref)


def _moe_ffn(cfg, tl, h, w1, w3, w2, layer, tile_expert, n_used, src_tok):
    tm, tf, d = tl.tm, tl.tf, cfg.d_model
    n_tiles_max, _, tm_pad = src_tok.shape
    nf = cfg.d_ff // tf
    assert tm_pad % nf == 0 and tm_pad >= tm

    def live_tile(t, nu):
        return jnp.minimum(t, nu[0] - 1)

    def f_idx(t, f, nu):
        return jnp.where(t < nu[0], f, nf - 1)

    grid_spec = pltpu.PrefetchScalarGridSpec(
        num_scalar_prefetch=2,
        grid=(n_tiles_max, nf),
        in_specs=[
            pl.BlockSpec((1, 1, tm_pad), lambda t, f, te, nu: (0, 0, 0), memory_space=pltpu.SMEM),
            pl.BlockSpec((1, 1, tm_pad), lambda t, f, te, nu: (live_tile(t + 1, nu), 0, 0),
                         memory_space=pltpu.SMEM),
            pl.BlockSpec(memory_space=pl.ANY),
            pl.BlockSpec((None, 1, d, tf), lambda t, f, te, nu: (layer, te[t], 0, f_idx(t, f, nu))),
            pl.BlockSpec((None, 1, d, tf), lambda t, f, te, nu: (layer, te[t], 0, f_idx(t, f, nu))),
            pl.BlockSpec((None, 1, tf, d), lambda t, f, te, nu: (layer, te[t], f_idx(t, f, nu), 0)),
        ],
        out_specs=pl.BlockSpec((tm, d), lambda t, f, te, nu: (live_tile(t, nu), 0)),
        scratch_shapes=[pltpu.VMEM((2, tm_pad, d), F32), pltpu.VMEM((tm, d), BF16),
                        pltpu.SemaphoreType.DMA((2,))],
    )
    return pl.pallas_call(
        functools.partial(_moe_ffn_kernel, tm=tm, rows_per_step=tm_pad // nf),
        grid_spec=grid_spec,
        out_shape=jax.ShapeDtypeStruct((n_tiles_max * tm, d), F32),
        compiler_params=_params("arbitrary", "arbitrary"),
        name="moe_ffn",
    )(tile_expert, n_used, src_tok, src_tok, h, w1, w3, w2)


def _moe_combine_kernel(pa0_ref, pb0_ref, pa1_ref, pb1_ref, y_hbm, x_ref, gate_ref, route_ref, o_ref,
                        ya, yb, sem, *, tm):
    i = pl.program_id(0)
    slot = lax.rem(i, 2)

    def start(pa_ref, pb_ref, s):
        _start_row_gather(pa_ref, y_hbm, ya.at[s], sem.at[s], tm)
        _start_row_gather(pb_ref, y_hbm, yb.at[s], sem.at[s], tm)

    @pl.when(i == 0)
    def _():
        start(pa0_ref, pb0_ref, 0)

    @pl.when(i + 1 < pl.num_programs(0))
    def _():
        start(pa1_ref, pb1_ref, 1 - slot)

    _wait_row_gather(y_hbm, ya.at[slot], sem.at[slot], tm)
    _wait_row_gather(y_hbm, yb.at[slot], sem.at[slot], tm)
    route = route_ref[...]
    y = route[:, 0:1] * ya[slot] + route[:, 1:2] * yb[slot]
    o_ref[...] = x_ref[...] + gate_ref[0, 0] * y


def _moe_combine(cfg, tl, y_sorted, x_all, mod, route, pos_a, pos_b, n_tiles):
    tm, d = tl.tm, cfg.d_model
    first = pl.BlockSpec((1, 1, tm), lambda i: (0, 0, 0), memory_space=pltpu.SMEM)
    nxt = pl.BlockSpec((1, 1, tm), lambda i: (jnp.minimum(i + 1, n_tiles - 1), 0, 0), memory_space=pltpu.SMEM)
    return pl.pallas_call(
        functools.partial(_moe_combine_kernel, tm=tm),
        grid=(n_tiles,),
        in_specs=[
            first, first, nxt, nxt,
            pl.BlockSpec(memory_space=pl.ANY),
            pl.BlockSpec((tm, d), lambda i: (i, 0)),
            _mod_spec(cfg, tm, 5, d, 1),
            pl.BlockSpec((tm, LANE), lambda i: (i, 0)),
        ],
        out_specs=pl.BlockSpec((tm, d), lambda i: (i, 0)),
        out_shape=jax.ShapeDtypeStruct((n_tiles * tm, d), F32),
        scratch_shapes=[pltpu.VMEM((2, tm, d), F32), pltpu.VMEM((2, tm, d), F32), pltpu.SemaphoreType.DMA((2,))],
        compiler_params=_params("arbitrary"),
        name="moe_combine",
    )(pos_a, pos_b, pos_a, pos_b, y_sorted, x_all, mod, route)


def _rope_tables(cfg, tm):
    half = cfg.head_dim // 2
    quarter = half // 2
    t = jnp.arange(cfg.seq)
    freqs = ROPE_THETA ** (-jnp.arange(quarter, dtype=F32) / quarter)
    ang_r = (t // cfg.grid_w).astype(F32)[:, None] * freqs[None, :]
    ang_c = (t % cfg.grid_w).astype(F32)[:, None] * freqs[None, :]
    cos = jnp.concatenate([jnp.cos(ang_r), jnp.cos(ang_c)] * 2, axis=-1)
    sin = jnp.concatenate([-jnp.sin(ang_r), -jnp.sin(ang_c), jnp.sin(ang_r), jnp.sin(ang_c)], axis=-1)
    cos = jnp.concatenate([cos, jnp.ones((tm, cfg.head_dim), F32)], axis=0)
    sin = jnp.concatenate([sin, jnp.zeros((tm, cfg.head_dim), F32)], axis=0)
    return cos, sin


def _forward(cfg, x, c, ctx, c_ctx, ada_w, ada_b, norm_attn, norm_ffn, w_in, qn_a, kn_a, qn_b, kn_b, rpb,
             w_out, w1_dense, w3_dense, w2_dense, w_router, w1_moe, w3_moe, w2_moe):
    tl = _pick_tiles(cfg)
    d = cfg.d_model
    n_tiles = cfg.n_tok // tl.tm
    n_lat_tiles = cfg.n_lat // tl.tm

    x_all = jnp.concatenate([x.reshape(cfg.n_lat, d), ctx.reshape(cfg.batch * cfg.ctx_len, d)], axis=0)

    cond = jnp.concatenate([c, c_ctx[None, :]], axis=0)
    pad_rows = -(-cond.shape[0] // 16) * 16
    cond_pad = jnp.zeros((pad_rows, d), F32).at[:cond.shape[0]].set(cond)
    mod_all = _ada_mod(cfg, tl, cond_pad, ada_w, ada_b)

    rope_cos, rope_sin = _rope_tables(cfg, tl.tm_proj)
    win, starts, classes, row_tables, col_table = _na_geometry(cfg, tl.rb)
    na_bias = _na_bias(rpb, row_tables, col_table)
    q_scale = cfg.head_dim ** -0.5 * LOG2E
    ones_a = jnp.ones((cfg.wa_kv,), F32)
    ones_b = jnp.ones((cfg.wb,), F32)
    w_in, w_out = w_in.astype(BF16), w_out.astype(BF16)
    def rotary_cols(w):
        return _rotary_layout(w.reshape(w.shape[:2] + (-1, cfg.head_dim)), 3).reshape(w.shape)
    w_in = jnp.concatenate([rotary_cols(w_in[:, :, :cfg.wa_q]), w_in[:, :, cfg.wa_q:cfg.off_ka],
                            rotary_cols(w_in[:, :, cfg.off_ka:cfg.off_va]), w_in[:, :, cfg.off_va:]], axis=2)
    qn_a, kn_a = _rotary_layout(qn_a, 1), _rotary_layout(kn_a, 1)
    w1_dense, w3_dense, w2_dense = (w.astype(BF16) for w in (w1_dense, w3_dense, w2_dense))
    w1_moe, w3_moe, w2_moe = (w.astype(BF16) for w in (w1_moe, w3_moe, w2_moe))
    nf = cfg.d_ff // tl.tf
    tm_pad = -(-tl.tm // (8 * nf)) * 8 * nf

    for i in range(cfg.depth):
        last = i == cfg.depth - 1
        j = i // 2
        mod = mod_all[i, :cfg.batch + 1].reshape(cfg.batch + 1, N_MOD, 1, d)
        gain = jnp.concatenate([
            jnp.tile(qn_a[i], cfg.heads_a) * q_scale, jnp.tile(qn_b[i], cfg.heads_b) * q_scale,
            jnp.tile(kn_a[i], cfg.kv_a), ones_a, jnp.tile(kn_b[i], cfg.heads_b), ones_b])[None, :]

        p = _in_proj(cfg, tl, x_all, norm_attn[i][None, :], mod, w_in, i, gain, rope_cos, rope_sin)
        o = _attn_a(cfg, tl, p)
        o = _attn_b(cfg, tl, p, o, na_bias, i, starts, classes, win)
        if not last:
            o = _attn_ctx(cfg, p, o)
        live_tiles = n_lat_tiles if last else n_tiles
        x_all = _out_proj(cfg, tl, o, w_out, i, x_all, mod, live_tiles * tl.tm)

        g_ffn = norm_ffn[i][None, :]
        if i % 2 == 0:
            x_all = _ffn(cfg, tl, x_all, g_ffn, mod, w1_dense, w3_dense, w2_dense, j, live_tiles)
        else:
            wr = jnp.zeros((d, LANE), F32).at[:, :cfg.n_experts].set(w_router[j])
            h, route = _router(cfg, tl, x_all, g_ffn, mod, wr, live_tiles)
            tile_expert, n_used, src_tok, pos_a, pos_b = _route_plan(cfg, tl.tm, tm_pad, route,
                                                                     live_tiles * tl.tm)
            y_sorted = _moe_ffn(cfg, tl, h, w1_moe, w3_moe, w2_moe, j, tile_expert, n_used, src_tok)
            x_all = _moe_combine(cfg, tl, y_sorted, x_all, mod, route, pos_a, pos_b, live_tiles)

    return x_all.reshape(cfg.batch, cfg.seq, d)


def kernel(x, c, ctx, c_ctx, ada_w, ada_b, norm_attn, norm_ffn, w_in, qn_a, kn_a, qn_b, kn_b, rpb, w_out,
           w1_dense, w3_dense, w2_dense, w_router, w1_moe, w3_moe, w2_moe):
    batch, seq, d_model = x.shape
    depth = w_in.shape[0]
    head_dim = qn_a.shape[-1]
    heads_b = rpb.shape[1]
    wb = heads_b * head_dim
    wa_q = w_out.shape[1] - wb
    wa_kv = (w_in.shape[2] - wa_q - 3 * wb) // 2
    cfg = Cfg(d_model=d_model, batch=batch, seq=seq, ctx_len=ctx.shape[1], grid_w=64, head_dim=head_dim,
              heads_a=wa_q // head_dim, kv_a=wa_kv // head_dim, heads_b=heads_b,
              na_kh=(rpb.shape[2] + 1) // 2, na_kw=(rpb.shape[3] + 1) // 2, d_ff=w1_dense.shape[-1],
              n_experts=w_router.shape[-1], depth=depth)
    return _forward(cfg, x, c, ctx, c_ctx, ada_w, ada_b, norm_attn, norm_ffn, w_in, qn_a, kn_a, qn_b, kn_b,
                    rpb, w_out, w1_dense, w3_dense, w2_dense, w_router, w1_moe, w3_moe, w2_moe)
```

```python
import functools
from typing import NamedTuple

import numpy as np
import jax
import jax.numpy as jnp
from jax import lax
from jax.experimental import pallas as pl
from jax.experimental.pallas import tpu as pltpu

F32 = jnp.float32
BF16 = jnp.bfloat16

EPS = 1e-6
ROPE_THETA = 10000.0
N_MOD = 6
LANE = 128
MASK_VALUE = -1e30
LOG2E = 1.4426950408889634
V7X_VMEM_LIMIT = 56 * 1024 * 1024


class Cfg(NamedTuple):
    d_model: int
    batch: int
    seq: int
    ctx_len: int
    grid_w: int
    head_dim: int
    heads_a: int
    kv_a: int
    heads_b: int
    na_kh: int
    na_kw: int
    d_ff: int
    n_experts: int
    depth: int

    @property
    def wa_q(self):
        return self.heads_a * self.head_dim

    @property
    def wa_kv(self):
        return self.kv_a * self.head_dim

    @property
    def wb(self):
        return self.heads_b * self.head_dim

    @property
    def mix_w(self):
        return self.wa_q + self.wb

    @property
    def in_cols(self):
        return self.mix_w + 2 * self.wa_kv + 2 * self.wb

    @property
    def rows(self):
        return self.seq // self.grid_w

    @property
    def n_lat(self):
        return self.batch * self.seq

    @property
    def n_tok(self):
        return self.batch * (self.seq + self.ctx_len)

    @property
    def off_ka(self):
        return self.mix_w

    @property
    def off_va(self):
        return self.mix_w + self.wa_kv

    @property
    def off_kb(self):
        return self.mix_w + 2 * self.wa_kv

    @property
    def off_vb(self):
        return self.mix_w + 2 * self.wa_kv + self.wb


class Tiles(NamedTuple):
    tm: int
    tm_proj: int
    hps: int
    tn_out: int
    tf: int
    tq: int
    rb: int
    tn_ada: int


def _pick_tiles(cfg):
    n_ctx = cfg.batch * cfg.ctx_len
    tm = min(512, cfg.seq, n_ctx)
    tm_proj = min(1024, cfg.seq, n_ctx)
    for t in (tm, tm_proj):
        assert cfg.seq % t == 0 and n_ctx % t == 0
    n_heads = cfg.in_cols // LANE
    hps = next(h for h in (12, 10, 8, 6, 4, 2) if n_heads % h == 0)
    tn_out = min(1024, cfg.d_model)
    tf = min(512, cfg.d_ff)
    assert cfg.d_model % tn_out == 0 and cfg.d_ff % tf == 0
    tq = min(256, cfg.seq)
    assert cfg.seq % tq == 0
    rb = 4
    assert cfg.rows % (2 * rb) == 0 and cfg.rows >= rb + cfg.na_kh
    tn_ada = next(t for t in (1024, 512, 256, LANE) if (N_MOD * cfg.d_model) % t == 0)
    return Tiles(tm, tm_proj, hps, tn_out, tf, tq, rb, tn_ada)


def _params(*sem):
    return pltpu.CompilerParams(dimension_semantics=sem, vmem_limit_bytes=V7X_VMEM_LIMIT)


def _dot(a, b):
    return jnp.dot(a, b, preferred_element_type=F32)


def _dot_nt(a, b):
    return lax.dot_general(a, b, (((1,), (1,)), ((), ())), preferred_element_type=F32)


def _rms(x):
    return x * lax.rsqrt(jnp.mean(x * x, axis=-1, keepdims=True) + EPS)


def _ada_kernel(cond_ref, w_ref, b_ref, o_ref):
    c = cond_ref[...]
    c = (c * jax.nn.sigmoid(c)).astype(BF16)
    o_ref[0] = _dot(c, w_ref[0].astype(BF16)) + b_ref[0]


def _ada_mod(cfg, tl, cond_pad, ada_w, ada_b):
    depth, d, n6 = ada_w.shape
    mp = cond_pad.shape[0]
    return pl.pallas_call(
        _ada_kernel,
        grid=(depth, n6 // tl.tn_ada),
        in_specs=[
            pl.BlockSpec((mp, d), lambda l, j: (0, 0)),
            pl.BlockSpec((1, d, tl.tn_ada), lambda l, j: (l, 0, j)),
            pl.BlockSpec((1, 1, tl.tn_ada), lambda l, j: (l, 0, j)),
        ],
        out_specs=pl.BlockSpec((1, mp, tl.tn_ada), lambda l, j: (l, 0, j)),
        out_shape=jax.ShapeDtypeStruct((depth, mp, n6), F32),
        compiler_params=_params("parallel", "parallel"),
        name="ada_mod",
    )(cond_pad, ada_w, ada_b.reshape(depth, 1, n6))


def _mod_index(cfg, tm):
    n_lat_tiles = cfg.n_lat // tm
    per_sample = cfg.seq // tm

    def f(i):
        return jnp.where(i < n_lat_tiles, i // per_sample, cfg.batch)
    return f


def _mod_spec(cfg, tm, k, ncols, grid_rank, col_axis=None):
    mi = _mod_index(cfg, tm)
    if col_axis is None:
        return pl.BlockSpec((1, 1, 1, ncols), lambda i, *rest: (mi(i), k, 0, 0))
    assert grid_rank == 2 and col_axis == 1
    return pl.BlockSpec((1, 1, 1, ncols), lambda i, j: (mi(i), k, 0, j))


def _modulated(x, g_ref, shift_ref, scale_ref):
    y = _rms(x) * g_ref[...]
    return y * (1.0 + scale_ref[0, 0]) + shift_ref[0, 0]


def _rotary_partner(y):
    return pltpu.roll(y, LANE // 2, 1)


def _rotary_layout(a, axis):
    shape = a.shape
    quarter = shape[axis] // 4
    a = a.reshape(shape[:axis] + (2, 2, quarter) + shape[axis + 1:])
    return jnp.swapaxes(a, axis, axis + 1).reshape(shape)


HEAD_ROPE, HEAD_NORM, HEAD_PLAIN = "rope", "norm", "plain"


def _proj_kernel(x_ref, g_ref, shift_ref, scale_ref, w_ref, gain_ref, cos_ref, sin_ref, o_ref, h_ref,
                 *, step_kinds):
    j = pl.program_id(1)

    @pl.when(j == 0)
    def _():
        h_ref[...] = _modulated(x_ref[...], g_ref, shift_ref, scale_ref).astype(BF16)

    def head_epilogue(y, kind, k):
        if kind != HEAD_PLAIN:
            y = _rms(y) * gain_ref[:, k * LANE:(k + 1) * LANE]
        if kind == HEAD_ROPE:
            y = y * cos_ref[...] + _rotary_partner(y) * sin_ref[...]
        return y

    def run(kinds):
        for c in range(0, len(kinds), 2):
            acc = _dot(h_ref[...], w_ref[:, c * LANE:(c + 2) * LANE])
            for k in (c, c + 1):
                y = head_epilogue(acc[:, (k - c) * LANE:(k - c + 1) * LANE], kinds[k], k)
                o_ref[:, k * LANE:(k + 1) * LANE] = y.astype(o_ref.dtype)

    for kinds in sorted(set(step_kinds)):
        steps = [s for s, kk in enumerate(step_kinds) if kk == kinds]
        cond = functools.reduce(jnp.logical_or, [j == s for s in steps])
        pl.when(cond)(functools.partial(run, kinds))


def _in_proj(cfg, tl, x_all, g, mod, w_in, layer, gain, rope_cos, rope_sin):
    tm, tn = tl.tm_proj, tl.hps * LANE
    d = cfg.d_model
    n_tiles = cfg.n_tok // tm
    n_lat_tiles = cfg.n_lat // tm
    per_sample = cfg.seq // tm
    head_kinds = ([HEAD_ROPE] * cfg.heads_a + [HEAD_NORM] * cfg.heads_b + [HEAD_ROPE] * cfg.kv_a
                  + [HEAD_PLAIN] * cfg.kv_a + [HEAD_NORM] * cfg.heads_b + [HEAD_PLAIN] * cfg.heads_b)
    step_kinds = tuple(tuple(head_kinds[s:s + tl.hps]) for s in range(0, len(head_kinds), tl.hps))
    kern = functools.partial(_proj_kernel, step_kinds=step_kinds)

    def rope_idx(i, j):
        return (jnp.where(i < n_lat_tiles, i % per_sample, per_sample), 0)

    return pl.pallas_call(
        kern,
        grid=(n_tiles, cfg.in_cols // tn),
        in_specs=[
            pl.BlockSpec((tm, d), lambda i, j: (i, 0)),
            pl.BlockSpec((1, d), lambda i, j: (0, 0)),
            _mod_spec(cfg, tm, 0, d, 2),
            _mod_spec(cfg, tm, 1, d, 2),
            pl.BlockSpec((None, d, tn), lambda i, j: (layer, 0, j)),
            pl.BlockSpec((1, tn), lambda i, j: (0, j)),
            pl.BlockSpec((tm, LANE), rope_idx),
            pl.BlockSpec((tm, LANE), rope_idx),
        ],
        out_specs=pl.BlockSpec((tm, tn), lambda i, j: (i, j)),
        out_shape=jax.ShapeDtypeStruct((cfg.n_tok, cfg.in_cols), BF16),
        scratch_shapes=[pltpu.VMEM((tm, d), BF16)],
        compiler_params=_params("parallel", "arbitrary"),
        name="in_proj",
    )(x_all, g, mod, mod, w_in, gain, rope_cos, rope_sin)


def _softmax_pv(s_parts, v_parts):
    m = None
    for s in s_parts:
        mx = jnp.max(s, axis=-1, keepdims=True)
        m = mx if m is None else jnp.maximum(m, mx)
    l = None
    o = None
    for s, v in zip(s_parts, v_parts):
        p = jnp.exp2(s - m)
        ps = jnp.sum(p, axis=-1, keepdims=True)
        pv = _dot(p.astype(BF16), v)
        l = ps if l is None else l + ps
        o = pv if o is None else o + pv
    return o / l


def _softmax_pv_ext(s_parts, vext_parts):
    m = None
    for s in s_parts:
        mx = jnp.max(s, axis=-1, keepdims=True)
        m = mx if m is None else jnp.maximum(m, mx)
    oe = None
    for s, v in zip(s_parts, vext_parts):
        pv = _dot(jnp.exp2(s - m).astype(BF16), v)
        oe = pv if oe is None else oe + pv
    return oe[:, :LANE] / oe[:, LANE:]


def _softmax_pv_given_max(s, m, v_ext):
    oe = _dot(jnp.exp2(s - m).astype(BF16), v_ext)
    return oe[:, :LANE] / oe[:, LANE:]


def _fill_kv_ext(k_all, v_ext, kl_ref, vl_ref, kc_ref, vc_ref):
    n_lat = kl_ref.shape[0]
    k_all[:n_lat, :] = kl_ref[...]
    k_all[n_lat:, :] = kc_ref[...]
    v_ext[:n_lat, :LANE] = vl_ref[...]
    v_ext[n_lat:, :LANE] = vc_ref[...]
    v_ext[:, LANE:] = jnp.ones((v_ext.shape[0], LANE), v_ext.dtype)


def _attn_a_kernel(q_ref, kl_ref, vl_ref, kc_ref, vc_ref, o_ref, k_all, v_ext, s_buf, m_buf, *, group):
    tq = q_ref.shape[0]
    phase = pl.program_id(3)

    @pl.when(jnp.logical_and(pl.program_id(2) == 0, phase == 0))
    def _():
        _fill_kv_ext(k_all, v_ext, kl_ref, vl_ref, kc_ref, vc_ref)

    @pl.when(phase == 0)
    def _():
        q = jnp.concatenate([q_ref[:, h * LANE:(h + 1) * LANE] for h in range(group)], axis=0)
        s = _dot_nt(q, k_all[...])
        s_buf[...] = s
        m_buf[...] = jnp.max(s, axis=-1, keepdims=True)

    @pl.when(phase == 1)
    def _():
        o = _softmax_pv_given_max(s_buf[...], m_buf[...], v_ext[...])
        for h in range(group):
            o_ref[:, h * LANE:(h + 1) * LANE] = o[h * tq:(h + 1) * tq].astype(o_ref.dtype)


def _attn_a(cfg, tl, p):
    assert cfg.head_dim == LANE
    group = cfg.heads_a // cfg.kv_a
    gw = group * LANE
    tq = tl.tq
    nq = cfg.seq // tq
    ctx_blk0 = cfg.n_lat // cfg.ctx_len
    ka0, va0 = cfg.off_ka // LANE, cfg.off_va // LANE
    n_keys = cfg.seq + cfg.ctx_len
    return pl.pallas_call(
        functools.partial(_attn_a_kernel, group=group),
        grid=(cfg.batch, cfg.kv_a, nq, 2),
        scratch_shapes=[pltpu.VMEM((n_keys, LANE), BF16), pltpu.VMEM((n_keys, 2 * LANE), BF16),
                        pltpu.VMEM((group * tq, n_keys), F32), pltpu.VMEM((group * tq, 1), F32)],
        in_specs=[
            pl.BlockSpec((tq, gw), lambda b, g, i, ph: (b * nq + i, g)),
            pl.BlockSpec((cfg.seq, LANE), lambda b, g, i, ph: (b, ka0 + g)),
            pl.BlockSpec((cfg.seq, LANE), lambda b, g, i, ph: (b, va0 + g)),
            pl.BlockSpec((cfg.ctx_len, LANE), lambda b, g, i, ph: (ctx_blk0 + b, ka0 + g)),
            pl.BlockSpec((cfg.ctx_len, LANE), lambda b, g, i, ph: (ctx_blk0 + b, va0 + g)),
        ],
        out_specs=pl.BlockSpec((tq, gw), lambda b, g, i, ph: (b * nq + i, g)),
        out_shape=jax.ShapeDtypeStruct((cfg.n_tok, cfg.mix_w), BF16),
        compiler_params=_params("arbitrary", "arbitrary", "arbitrary", "arbitrary"),
        name="attn_global",
    )(p, p, p, p, p)


def _na_geometry(cfg, rb):
    rows, w = cfg.rows, cfg.grid_w
    kh = min(cfg.na_kh, rows)
    kw = cfg.na_kw
    win = min(rows, -(-(rb - 1 + kh) // rb) * rb)
    nblk = rows // rb
    r0 = np.clip(np.arange(rows) - kh // 2, 0, rows - kh)
    c0 = np.clip(np.arange(w) - kw // 2, 0, w - kw)
    starts = np.clip(rb * np.arange(nblk) - kh // 2, 0, rows - win)
    col = np.arange(w)
    vc = (col[None, :] >= c0[:, None]) & (col[None, :] < c0[:, None] + kw)
    dc = np.where(vc, col[None, :] - col[:, None] + (cfg.na_kw - 1), 0)

    def row_table(i):
        r = rb * i + np.arange(rb)
        key_r = starts[i] + np.arange(win)
        vr = (key_r[None, :] >= r0[r][:, None]) & (key_r[None, :] < r0[r][:, None] + kh)
        assert (vr.sum(axis=1) == kh).all()
        return vr, np.where(vr, key_r[None, :] - r[:, None] + (cfg.na_kh - 1), 0)

    row_tables, classes = [], []
    for i in range(nblk):
        t = row_table(i)
        for c, u in enumerate(row_tables):
            if all(np.array_equal(a, b) for a, b in zip(t, u)):
                classes.append(c)
                break
        else:
            classes.append(len(row_tables))
            row_tables.append(t)
    return win, starts, np.asarray(classes), row_tables, (vc, dc)


def _na_bias(rpb, row_tables, col_table):
    depth, h, n_dr, n_dc = rpb.shape
    vc, dc = col_table
    w = vc.shape[0]
    onehot = (dc[None] == np.arange(n_dc)[:, None, None]).astype(np.float32)
    t = jnp.einsum("lhdk,kqc->lhdqc", rpb, onehot, precision=lax.Precision.HIGHEST)
    vr = np.stack([v for v, _ in row_tables])
    dr = np.stack([d for _, d in row_tables])
    ncls, rb, win = vr.shape
    b = jnp.take(t, dr.reshape(-1), axis=2).reshape(depth, h, ncls, rb, win, w, w)
    valid = vr[:, :, :, None, None] & vc[None, None, None]
    b = jnp.where(valid[None, None], b * LOG2E, MASK_VALUE).transpose(0, 1, 2, 3, 5, 4, 6)
    return b.reshape(depth, h, ncls, rb * w, win * w).astype(F32)


def _attn_b_kernel(start_ref, cls_ref, o_in_ref, q_ref, kl_ref, vl_ref, kc_ref, vc_ref, bias_ref, o_ref,
                   k_all, v_ext, s_buf, *, nblk, qb, kwin, w):
    del o_in_ref
    _fill_kv_ext(k_all, v_ext, kl_ref, vl_ref, kc_ref, vc_ref)
    n_lat = kl_ref.shape[0]

    def rows(i):
        return pl.ds(pl.multiple_of(i * qb, qb), qb)

    def window(i):
        return pl.ds(pl.multiple_of(start_ref[i] * w, w), kwin)

    def scores(i, carry):
        q = q_ref[rows(i), :]
        s_buf[i, :, :kwin] = _dot_nt(q, k_all[window(i), :]) + bias_ref[0, cls_ref[i]]
        s_buf[i, :, kwin:] = _dot_nt(q, k_all[n_lat:, :])
        return carry

    def outputs(i, carry):
        s = s_buf[i]
        p = jnp.exp2(s - jnp.max(s, axis=-1, keepdims=True)).astype(BF16)
        oe = _dot(p[:, :kwin], v_ext[window(i), :]) + _dot(p[:, kwin:], v_ext[n_lat:, :])
        o_ref[rows(i), :] = (oe[:, :LANE] / oe[:, LANE:]).astype(o_ref.dtype)
        return carry

    lax.fori_loop(0, nblk, scores, 0, unroll=4)
    lax.fori_loop(0, nblk, outputs, 0, unroll=4)


def _attn_b(cfg, tl, p, o, bias, layer, starts, classes, win):
    w = cfg.grid_w
    qb, kwin = tl.rb * w, win * w
    nblk = cfg.rows // tl.rb
    ncls = bias.shape[2]
    ctx_blk0 = cfg.n_lat // cfg.ctx_len
    q0, k0, v0 = cfg.wa_q // LANE, cfg.off_kb // LANE, cfg.off_vb // LANE
    grid_spec = pltpu.PrefetchScalarGridSpec(
        num_scalar_prefetch=2,
        grid=(cfg.heads_b, cfg.batch),
        in_specs=[
            pl.BlockSpec(memory_space=pl.ANY),
            pl.BlockSpec((cfg.seq, LANE), lambda h, b, s, c: (b, q0 + h)),
            pl.BlockSpec((cfg.seq, LANE), lambda h, b, s, c: (b, k0 + h)),
            pl.BlockSpec((cfg.seq, LANE), lambda h, b, s, c: (b, v0 + h)),
            pl.BlockSpec((cfg.ctx_len, LANE), lambda h, b, s, c: (ctx_blk0 + b, k0 + h)),
            pl.BlockSpec((cfg.ctx_len, LANE), lambda h, b, s, c: (ctx_blk0 + b, v0 + h)),
            pl.BlockSpec((None, 1, ncls, qb, kwin), lambda h, b, s, c: (layer, h, 0, 0, 0)),
        ],
        out_specs=pl.BlockSpec((cfg.seq, LANE), lambda h, b, s, c: (b, q0 + h)),
        scratch_shapes=[pltpu.VMEM((cfg.seq + cfg.ctx_len, LANE), BF16),
                        pltpu.VMEM((cfg.seq + cfg.ctx_len, 2 * LANE), BF16),
                        pltpu.VMEM((nblk, qb, kwin + cfg.ctx_len), F32)],
    )
    return pl.pallas_call(
        functools.partial(_attn_b_kernel, nblk=nblk, qb=qb, kwin=kwin, w=w),
        grid_spec=grid_spec,
        out_shape=jax.ShapeDtypeStruct(o.shape, o.dtype),
        input_output_aliases={2: 0},
        compiler_params=_params("parallel", "arbitrary"),
        name="attn_neighbourhood",
    )(jnp.asarray(starts, jnp.int32), jnp.asarray(classes, jnp.int32), o, p, p, p, p, p, bias)


def _attn_ctx_kernel(o_in_ref, p_ref, o_ref, *, cfg):
    del o_in_ref
    group = cfg.heads_a // cfg.kv_a

    def cols(off, h):
        return p_ref[:, off + h * LANE: off + (h + 1) * LANE]

    for h in range(cfg.heads_a):
        g = h // group
        o = _softmax_pv([_dot_nt(cols(0, h), cols(cfg.off_ka, g))], [cols(cfg.off_va, g)])
        o_ref[:, h * LANE:(h + 1) * LANE] = o.astype(o_ref.dtype)
    for h in range(cfg.heads_b):
        o = _softmax_pv([_dot_nt(cols(cfg.wa_q, h), cols(cfg.off_kb, h))], [cols(cfg.off_vb, h)])
        o_ref[:, cfg.wa_q + h * LANE: cfg.wa_q + (h + 1) * LANE] = o.astype(o_ref.dtype)


def _attn_ctx(cfg, p, o):
    ctx_blk0 = cfg.n_lat // cfg.ctx_len
    return pl.pallas_call(
        functools.partial(_attn_ctx_kernel, cfg=cfg),
        grid=(cfg.batch,),
        in_specs=[
            pl.BlockSpec(memory_space=pl.ANY),
            pl.BlockSpec((cfg.ctx_len, cfg.in_cols), lambda b: (ctx_blk0 + b, 0)),
        ],
        out_specs=pl.BlockSpec((cfg.ctx_len, cfg.mix_w), lambda b: (ctx_blk0 + b, 0)),
        out_shape=jax.ShapeDtypeStruct(o.shape, o.dtype),
        input_output_aliases={0: 0},
        compiler_params=_params("parallel"),
        name="attn_context",
    )(o, p)


def _out_proj_kernel(o_ref, w_ref, x_ref, gate_ref, y_ref):
    half = y_ref.shape[1] // 2
    gate = gate_ref[0, 0]
    for c in (slice(0, half), slice(half, 2 * half)):
        y_ref[:, c] = x_ref[:, c] + gate[:, c] * _dot(o_ref[...], w_ref[:, c])


def _out_proj(cfg, tl, o, w_out, layer, x_all, mod, n_rows):
    tm, tn = tl.tm_proj, tl.tn_out
    n_tiles = n_rows // tm
    return pl.pallas_call(
        _out_proj_kernel,
        grid=(n_tiles, cfg.d_model // tn),
        in_specs=[
            pl.BlockSpec((tm, cfg.mix_w), lambda i, j: (i, 0)),
            pl.BlockSpec((None, cfg.mix_w, tn), lambda i, j: (layer, 0, j)),
            pl.BlockSpec((tm, tn), lambda i, j: (i, j)),
            _mod_spec(cfg, tm, 2, tn, 2, col_axis=1),
        ],
        out_specs=pl.BlockSpec((tm, tn), lambda i, j: (i, j)),
        out_shape=jax.ShapeDtypeStruct((n_rows, cfg.d_model), F32),
        compiler_params=_params("parallel", "arbitrary"),
        name="out_proj",
    )(o, w_out, x_all, mod)


def _router_kernel(x_ref, g_ref, shift_ref, scale_ref, wr_ref, h_ref, route_ref, *, n_experts):
    h = _modulated(x_ref[...], g_ref, shift_ref, scale_ref)
    h_ref[...] = h
    logits = jnp.dot(h, wr_ref[...], preferred_element_type=F32, precision=lax.Precision.HIGHEST)
    lane = lax.broadcasted_iota(jnp.int32, logits.shape, 1).astype(F32)
    neg = jnp.float32(-jnp.inf)
    logits = jnp.where(lane < n_experts, logits, neg)
    v1 = jnp.max(logits, axis=-1, keepdims=True)
    i1 = jnp.min(jnp.where(logits == v1, lane, float(LANE)), axis=-1, keepdims=True)
    rest = jnp.where(lane == i1, neg, logits)
    v2 = jnp.max(rest, axis=-1, keepdims=True)
    i2 = jnp.min(jnp.where(rest == v2, lane, float(LANE)), axis=-1, keepdims=True)
    w1 = 1.0 / (1.0 + jnp.exp(v2 - v1))
    route = jnp.where(lane == 0.0, w1, jnp.where(lane == 1.0, 1.0 - w1, jnp.where(lane == 2.0, i1, i2)))
    route_ref[...] = jnp.where(lane < 4.0, route, 0.0)


def _router(cfg, tl, x_all, g, mod, w_router_pad, n_tiles):
    tm, d = tl.tm, cfg.d_model
    return pl.pallas_call(
        functools.partial(_router_kernel, n_experts=cfg.n_experts),
        grid=(n_tiles,),
        in_specs=[
            pl.BlockSpec((tm, d), lambda i: (i, 0)),
            pl.BlockSpec((1, d), lambda i: (0, 0)),
            _mod_spec(cfg, tm, 3, d, 1),
            _mod_spec(cfg, tm, 4, d, 1),
            pl.BlockSpec((d, LANE), lambda i: (0, 0)),
        ],
        out_specs=[pl.BlockSpec((tm, d), lambda i: (i, 0)), pl.BlockSpec((tm, LANE), lambda i: (i, 0))],
        out_shape=[jax.ShapeDtypeStruct((n_tiles * tm, d), F32), jax.ShapeDtypeStruct((n_tiles * tm, LANE), F32)],
        compiler_params=_params("parallel"),
        name="router",
    )(x_all, g, mod, mod, w_router_pad)


def _swiglu_step(h, w1_ref, w3_ref, w2_ref):
    a = _dot(h, w1_ref[0])
    b = _dot(h, w3_ref[0])
    return _dot((a * jax.nn.sigmoid(a) * b).astype(BF16), w2_ref[0])


def _ffn_kernel(x_ref, g_ref, shift_ref, scale_ref, gate_ref, w1_ref, w3_ref, w2_ref, y_ref, h_ref):
    f = pl.program_id(1)

    @pl.when(f == 0)
    def _():
        h_ref[...] = _modulated(x_ref[...], g_ref, shift_ref, scale_ref).astype(BF16)
        y_ref[...] = jnp.zeros_like(y_ref)

    y_ref[...] += _swiglu_step(h_ref[...], w1_ref, w3_ref, w2_ref)

    @pl.when(f == pl.num_programs(1) - 1)
    def _():
        y_ref[...] = x_ref[...] + gate_ref[0, 0] * y_ref[...]


def _ffn(cfg, tl, x_all, g, mod, w1, w3, w2, layer, n_tiles):
    tm, tf, d = tl.tm, tl.tf, cfg.d_model
    return pl.pallas_call(
        _ffn_kernel,
        grid=(n_tiles, cfg.d_ff // tf),
        in_specs=[
            pl.BlockSpec((tm, d), lambda i, f: (i, 0)),
            pl.BlockSpec((1, d), lambda i, f: (0, 0)),
            _mod_spec(cfg, tm, 3, d, 2),
            _mod_spec(cfg, tm, 4, d, 2),
            _mod_spec(cfg, tm, 5, d, 2),
            pl.BlockSpec((1, d, tf), lambda i, f: (layer, 0, f)),
            pl.BlockSpec((1, d, tf), lambda i, f: (layer, 0, f)),
            pl.BlockSpec((1, tf, d), lambda i, f: (layer, f, 0)),
        ],
        out_specs=pl.BlockSpec((tm, d), lambda i, f: (i, 0)),
        out_shape=jax.ShapeDtypeStruct((n_tiles * tm, d), F32),
        scratch_shapes=[pltpu.VMEM((tm, d), BF16)],
        compiler_params=_params("parallel", "arbitrary"),
        name="ffn_dense",
    )(x_all, g, mod, mod, mod, w1, w3, w2)


def _route_plan(cfg, tm, tm_pad, route, n_rows):
    n_e = cfg.n_experts
    n_assign = 2 * n_rows
    n_tiles_max = n_assign // tm + n_e + GATHER_AHEAD
    expert = route[:, 2:4].astype(jnp.int32).reshape(n_assign)
    onehot = (expert[:, None] == jnp.arange(n_e)[None, :]).astype(jnp.int32)
    csum = jnp.cumsum(onehot, axis=0)
    rank = jnp.sum(csum * onehot, axis=1) - 1
    counts = csum[-1]
    tiles_per_e = (counts + tm - 1) // tm
    tile_end = jnp.cumsum(tiles_per_e)
    tile_start = tile_end - tiles_per_e
    n_used = tile_end[-1]
    pos = jnp.sum(onehot * tile_start[None, :], axis=1) * tm + rank
    tile_ids = jnp.arange(n_tiles_max)
    tile_expert = jnp.minimum(jnp.sum(tile_ids[:, None] >= tile_end[None, :], axis=1), n_e - 1)
    tile_expert = jnp.where(tile_ids < n_used, tile_expert, tile_expert[jnp.maximum(n_used - 1, 0)])
    order = jnp.argsort(expert, stable=True)
    unpadded_start = jnp.cumsum(counts) - counts
    k = (tile_ids - tile_start[tile_expert])[:, None] * tm + jnp.arange(tm)[None, :]
    valid = jnp.logical_and(k < counts[tile_expert][:, None], (tile_ids < n_used)[:, None])
    src = order[jnp.clip(unpadded_start[tile_expert][:, None] + k, 0, n_assign - 1)] // 2
    src_tok = jnp.where(valid, src, 0).astype(jnp.int32)
    src_tok = jnp.pad(src_tok, ((0, 0), (0, tm_pad - tm))).reshape(n_tiles_max, 1, tm_pad)
    pos = pos.astype(jnp.int32).reshape(n_rows // tm, 1, tm, 2)
    return (tile_expert.astype(jnp.int32), n_used.astype(jnp.int32).reshape(1), src_tok,
            pos[..., 0], pos[..., 1])


def _start_row_gather(idx_ref, src_hbm, dst_ref, sem, n_rows):
    def body(r, carry):
        pltpu.make_async_copy(src_hbm.at[pl.ds(idx_ref[0, 0, r], 1), :], dst_ref.at[pl.ds(r, 1), :], sem).start()
        return carry
    lax.fori_loop(0, n_rows, body, 0, unroll=8)


def _wait_row_gather(src_hbm, dst_ref, sem, n_rows):
    assert dst_ref.shape[0] == n_rows
    pltpu.make_async_copy(src_hbm.at[pl.ds(0, n_rows), :], dst_ref, sem).wait()


GATHER_AHEAD = 2


def _moe_ffn_kernel(te_ref, nu_ref, src0_ref, src1_ref, src_ahead_ref, h_hbm, w1_ref, w3_ref, w2_ref, y_ref,
                    hbuf, h_ref, sem, *, tm, rows_per_step):
    del te_ref
    t, f = pl.program_id(0), pl.program_id(1)
    n_used = nu_ref[0]
    n_buf = GATHER_AHEAD + 1
    slot = lax.rem(t, n_buf)
    ahead = lax.rem(t + GATHER_AHEAD, n_buf)
    live = t < n_used
    at_tile_start = f == 0
    tm_pad = hbuf.shape[1]

    @pl.when(jnp.logical_and(at_tile_start, t == 0))
    def _():
        _start_row_gather(src0_ref, h_hbm, hbuf.at[0], sem.at[0], tm_pad)
        _start_row_gather(src1_ref, h_hbm, hbuf.at[1], sem.at[1], tm_pad)

    @pl.when(jnp.logical_and(at_tile_start, t < n_used + GATHER_AHEAD))
    def _():
        _wait_row_gather(h_hbm, hbuf.at[slot], sem.at[slot], tm_pad)

    @pl.when(jnp.logical_and(at_tile_start, live))
    def _():
        h_ref[...] = hbuf[slot, :tm].astype(BF16)
        y_ref[...] = jnp.zeros_like(y_ref)

    @pl.when(live)
    def _():
        for k in range(rows_per_step):
            r = f * rows_per_step + k
            pltpu.make_async_copy(h_hbm.at[pl.ds(src_ahead_ref[0, 0, r], 1), :],
                                  hbuf.at[ahead, pl.ds(r, 1), :], sem.at[ahead]).start()
        y_ref[...] += _swiglu_step(h_ref[...], w1_ref, w3_ref, w2_ref)


def _moe_ffn(cfg, tl, h, w1, w3, w2, layer, tile_expert, n_used, src_tok):
    tm, tf, d = tl.tm, tl.tf, cfg.d_model
    n_tiles_max, _, tm_pad = src_tok.shape
    nf = cfg.d_ff // tf
    assert tm_pad % nf == 0 and tm_pad >= tm

    def live_tile(t, nu):
        return jnp.minimum(t, nu[0] - 1)

    def f_idx(t, f, nu):
        return jnp.where(t < nu[0], f, nf - 1)

    grid_spec = pltpu.PrefetchScalarGridSpec(
        num_scalar_prefetch=2,
        grid=(n_tiles_max, nf),
        in_specs=[
            pl.BlockSpec((1, 1, tm_pad), lambda t, f, te, nu: (0, 0, 0), memory_space=pltpu.SMEM),
            pl.BlockSpec((1, 1, tm_pad), lambda t, f, te, nu: (live_tile(1, nu), 0, 0), memory_space=pltpu.SMEM),
            pl.BlockSpec((1, 1, tm_pad), lambda t, f, te, nu: (live_tile(t + GATHER_AHEAD, nu), 0, 0),
                         memory_space=pltpu.SMEM),
            pl.BlockSpec(memory_space=pl.ANY),
            pl.BlockSpec((None, 1, d, tf), lambda t, f, te, nu: (layer, te[t], 0, f_idx(t, f, nu))),
            pl.BlockSpec((None, 1, d, tf), lambda t, f, te, nu: (layer, te[t], 0, f_idx(t, f, nu))),
            pl.BlockSpec((None, 1, tf, d), lambda t, f, te, nu: (layer, te[t], f_idx(t, f, nu), 0)),
        ],
        out_specs=pl.BlockSpec((tm, d), lambda t, f, te, nu: (live_tile(t, nu), 0)),
        scratch_shapes=[pltpu.VMEM((GATHER_AHEAD + 1, tm_pad, d), F32), pltpu.VMEM((tm, d), BF16),
                        pltpu.SemaphoreType.DMA((GATHER_AHEAD + 1,))],
    )
    return pl.pallas_call(
        functools.partial(_moe_ffn_kernel, tm=tm, rows_per_step=tm_pad // nf),
        grid_spec=grid_spec,
        out_shape=jax.ShapeDtypeStruct((n_tiles_max * tm, d), F32),
        compiler_params=_params("arbitrary", "arbitrary"),
        name="moe_ffn",
    )(tile_expert, n_used, src_tok, src_tok, src_tok, h, w1, w3, w2)


def _moe_combine_kernel(pa0_ref, pb0_ref, pa1_ref, pb1_ref, y_hbm, x_ref, gate_ref, route_ref, o_ref,
                        ya, yb, sem, *, tm):
    i = pl.program_id(0)
    slot = lax.rem(i, 2)

    def start(pa_ref, pb_ref, s):
        _start_row_gather(pa_ref, y_hbm, ya.at[s], sem.at[s], tm)
        _start_row_gather(pb_ref, y_hbm, yb.at[s], sem.at[s], tm)

    @pl.when(i == 0)
    def _():
        start(pa0_ref, pb0_ref, 0)

    @pl.when(i + 1 < pl.num_programs(0))
    def _():
        start(pa1_ref, pb1_ref, 1 - slot)

    _wait_row_gather(y_hbm, ya.at[slot], sem.at[slot], tm)
    _wait_row_gather(y_hbm, yb.at[slot], sem.at[slot], tm)
    route = route_ref[...]
    y = route[:, 0:1] * ya[slot] + route[:, 1:2] * yb[slot]
    o_ref[...] = x_ref[...] + gate_ref[0, 0] * y


def _moe_combine(cfg, tl, y_sorted, x_all, mod, route, pos_a, pos_b, n_tiles):
    tm, d = tl.tm, cfg.d_model
    first = pl.BlockSpec((1, 1, tm), lambda i: (0, 0, 0), memory_space=pltpu.SMEM)
    nxt = pl.BlockSpec((1, 1, tm), lambda i: (jnp.minimum(i + 1, n_tiles - 1), 0, 0), memory_space=pltpu.SMEM)
    return pl.pallas_call(
        functools.partial(_moe_combine_kernel, tm=tm),
        grid=(n_tiles,),
        in_specs=[
            first, first, nxt, nxt,
            pl.BlockSpec(memory_space=pl.ANY),
            pl.BlockSpec((tm, d), lambda i: (i, 0)),
            _mod_spec(cfg, tm, 5, d, 1),
            pl.BlockSpec((tm, LANE), lambda i: (i, 0)),
        ],
        out_specs=pl.BlockSpec((tm, d), lambda i: (i, 0)),
        out_shape=jax.ShapeDtypeStruct((n_tiles * tm, d), F32),
        scratch_shapes=[pltpu.VMEM((2, tm, d), F32), pltpu.VMEM((2, tm, d), F32), pltpu.SemaphoreType.DMA((2,))],
        compiler_params=_params("arbitrary"),
        name="moe_combine",
    )(pos_a, pos_b, pos_a, pos_b, y_sorted, x_all, mod, route)


def _rope_tables(cfg, tm):
    half = cfg.head_dim // 2
    quarter = half // 2
    t = jnp.arange(cfg.seq)
    freqs = ROPE_THETA ** (-jnp.arange(quarter, dtype=F32) / quarter)
    ang_r = (t // cfg.grid_w).astype(F32)[:, None] * freqs[None, :]
    ang_c = (t % cfg.grid_w).astype(F32)[:, None] * freqs[None, :]
    cos = jnp.concatenate([jnp.cos(ang_r), jnp.cos(ang_c)] * 2, axis=-1)
    sin = jnp.concatenate([-jnp.sin(ang_r), -jnp.sin(ang_c), jnp.sin(ang_r), jnp.sin(ang_c)], axis=-1)
    cos = jnp.concatenate([cos, jnp.ones((tm, cfg.head_dim), F32)], axis=0)
    sin = jnp.concatenate([sin, jnp.zeros((tm, cfg.head_dim), F32)], axis=0)
    return cos, sin


def _forward(cfg, x, c, ctx, c_ctx, ada_w, ada_b, norm_attn, norm_ffn, w_in, qn_a, kn_a, qn_b, kn_b, rpb,
             w_out, w1_dense, w3_dense, w2_dense, w_router, w1_moe, w3_moe, w2_moe):
    tl = _pick_tiles(cfg)
    d = cfg.d_model
    n_tiles = cfg.n_tok // tl.tm
    n_lat_tiles = cfg.n_lat // tl.tm

    x_all = jnp.concatenate([x.reshape(cfg.n_lat, d), ctx.reshape(cfg.batch * cfg.ctx_len, d)], axis=0)

    cond = jnp.concatenate([c, c_ctx[None, :]], axis=0)
    pad_rows = -(-cond.shape[0] // 16) * 16
    cond_pad = jnp.zeros((pad_rows, d), F32).at[:cond.shape[0]].set(cond)
    mod_all = _ada_mod(cfg, tl, cond_pad, ada_w, ada_b)

    rope_cos, rope_sin = _rope_tables(cfg, tl.tm_proj)
    win, starts, classes, row_tables, col_table = _na_geometry(cfg, tl.rb)
    na_bias = _na_bias(rpb, row_tables, col_table)
    q_scale = cfg.head_dim ** -0.5 * LOG2E
    ones_a = jnp.ones((cfg.wa_kv,), F32)
    ones_b = jnp.ones((cfg.wb,), F32)
    w_in, w_out = w_in.astype(BF16), w_out.astype(BF16)
    def rotary_cols(w):
        return _rotary_layout(w.reshape(w.shape[:2] + (-1, cfg.head_dim)), 3).reshape(w.shape)
    w_in = jnp.concatenate([rotary_cols(w_in[:, :, :cfg.wa_q]), w_in[:, :, cfg.wa_q:cfg.off_ka],
                            rotary_cols(w_in[:, :, cfg.off_ka:cfg.off_va]), w_in[:, :, cfg.off_va:]], axis=2)
    qn_a, kn_a = _rotary_layout(qn_a, 1), _rotary_layout(kn_a, 1)
    w1_dense, w3_dense, w2_dense = (w.astype(BF16) for w in (w1_dense, w3_dense, w2_dense))
    w1_moe, w3_moe, w2_moe = (w.astype(BF16) for w in (w1_moe, w3_moe, w2_moe))
    nf = cfg.d_ff // tl.tf
    tm_pad = -(-tl.tm // (8 * nf)) * 8 * nf

    for i in range(cfg.depth):
        last = i == cfg.depth - 1
        j = i // 2
        mod = mod_all[i, :cfg.batch + 1].reshape(cfg.batch + 1, N_MOD, 1, d)
        gain = jnp.concatenate([
            jnp.tile(qn_a[i], cfg.heads_a) * q_scale, jnp.tile(qn_b[i], cfg.heads_b) * q_scale,
            jnp.tile(kn_a[i], cfg.kv_a), ones_a, jnp.tile(kn_b[i], cfg.heads_b), ones_b])[None, :]

        p = _in_proj(cfg, tl, x_all, norm_attn[i][None, :], mod, w_in, i, gain, rope_cos, rope_sin)
        o = _attn_a(cfg, tl, p)
        o = _attn_b(cfg, tl, p, o, na_bias, i, starts, classes, win)
        if not last:
            o = _attn_ctx(cfg, p, o)
        live_tiles = n_lat_tiles if last else n_tiles
        x_all = _out_proj(cfg, tl, o, w_out, i, x_all, mod, live_tiles * tl.tm)

        g_ffn = norm_ffn[i][None, :]
        if i % 2 == 0:
            x_all = _ffn(cfg, tl, x_all, g_ffn, mod, w1_dense, w3_dense, w2_dense, j, live_tiles)
        else:
            wr = jnp.zeros((d, LANE), F32).at[:, :cfg.n_experts].set(w_router[j])
            h, route = _router(cfg, tl, x_all, g_ffn, mod, wr, live_tiles)
            tile_expert, n_used, src_tok, pos_a, pos_b = _route_plan(cfg, tl.tm, tm_pad, route,
                                                                     live_tiles * tl.tm)
            y_sorted = _moe_ffn(cfg, tl, h, w1_moe, w3_moe, w2_moe, j, tile_expert, n_used, src_tok)
            x_all = _moe_combine(cfg, tl, y_sorted, x_all, mod, route, pos_a, pos_b, live_tiles)

    return x_all.reshape(cfg.batch, cfg.seq, d)


def kernel(x, c, ctx, c_ctx, ada_w, ada_b, norm_attn, norm_ffn, w_in, qn_a, kn_a, qn_b, kn_b, rpb, w_out,
           w1_dense, w3_dense, w2_dense, w_router, w1_moe, w3_moe, w2_moe):
    batch, seq, d_model = x.shape
    depth = w_in.shape[0]
    head_dim = qn_a.shape[-1]
    heads_b = rpb.shape[1]
    wb = heads_b * head_dim
    wa_q = w_out.shape[1] - wb
    wa_kv = (w_in.shape[2] - wa_q - 3 * wb) // 2
    cfg = Cfg(d_model=d_model, batch=batch, seq=seq, ctx_len=ctx.shape[1], grid_w=64, head_dim=head_dim,
              heads_a=wa_q // head_dim, kv_a=wa_kv // head_dim, heads_b=heads_b,
              na_kh=(rpb.shape[2] + 1) // 2, na_kw=(rpb.shape[3] + 1) // 2, d_ff=w1_dense.shape[-1],
              n_experts=w_router.shape[-1], depth=depth)
    return _forward(cfg, x, c, ctx, c_ctx, ada_w, ada_b, norm_attn, norm_ffn, w_in, qn_a, kn_a, qn_b, kn_b,
                    rpb, w_out, w1_dense, w3_dense, w2_dense, w_router, w1_moe, w3_moe, w2_moe)
```
